```python
import jax, jax.numpy as jnp
from jax import lax
import numpy as np

D_MODEL = 1024
BATCH = 8
SEQ = 2048
DEPTH = 2

HEAD_DIM = 64
NORM_EPS = 1e-6
D_FF = 2752
A_HEADS = 6
A_WIDTH = 384
A_DECAY_LORA = 32
A_AAA_LORA = 32
A_GATE_LORA = 64
A_MV_LORA = 16
A_GN_EPS = 64e-5
A_SLAB = (384, 384, 384, 32, 32, 64)
A_IN = 1280
B_HEADS = 6
B_WIDTH = 384
B_KV_LATENT = 128
B_IDX_HEADS = 8
B_IDX_DIM = 64
B_TOPK_MAX = 256
B_QUERY_BLOCK = 128
B_SLAB = (384, 128, 512, 64, 8)
B_IN = 1096
C_HEADS = 4
C_WIDTH = 256
C_CHUNK = 128
C_ROPE_BASE = 10000.0
C_SLAB = (256, 256, 256, 256)
C_IN = 1024
D_MIX = 1024
N_IN = 3400

kernel_name = 'hybrid_rwkv7_dsa_retention_macaron'


def _split(p, widths):
    out, start = [], 0
    for w in widths:
        out.append(p[..., start:start + w])
        start += w
    return out


def rmsnorm(x, g, eps=NORM_EPS):
    xf = x.astype(jnp.float32)
    y = xf * lax.rsqrt(jnp.mean(xf * xf, axis=-1, keepdims=True) + eps)
    return y.astype(x.dtype) * g


def swiglu(h, w_gate, w_up, w_down):
    return (jax.nn.silu(h @ w_gate) * (h @ w_up)) @ w_down


def shift_lerp(p, mu):
    prev = jnp.pad(p, ((0, 0), (1, 0), (0, 0)))[:, :-1]
    return p + (prev - p) * mu


def _head_layernorm(y, eps):
    mu = jnp.mean(y, axis=-1, keepdims=True)
    var = jnp.mean(jnp.square(y - mu), axis=-1, keepdims=True)
    return (y - mu) * lax.rsqrt(var + eps)


def _rwkv7_recurrence(r, decay, k, v, a, b):
    Bsz, T, H, N = r.shape

    def step(S, inp):
        r_t, w_t, k_t, v_t, a_t, b_t = inp
        sa = jnp.einsum('bhvk,bhk->bhv', S, a_t)
        S = S * w_t[:, :, None, :] + sa[..., None] * b_t[:, :, None, :] + v_t[..., None] * k_t[:, :, None, :]
        return S, jnp.einsum('bhvk,bhk->bhv', S, r_t)

    xs = tuple(jnp.moveaxis(z, 1, 0) for z in (r, decay, k, v, a, b))
    _, y = lax.scan(step, jnp.zeros((Bsz, H, N, N), jnp.float32), xs)
    return jnp.moveaxis(y, 0, 1)


def rwkv7_mixer(r, k, v, w_lo, a_lo, g_lo, w0, w2, a0, a2, g2, k_k, k_a, r_k, ln_w, ln_b):
    Bsz, T, _ = r.shape
    dt = r.dtype
    heads = lambda z: z.astype(jnp.float32).reshape(Bsz, T, A_HEADS, HEAD_DIM)
    w = -jax.nn.softplus(-(w0 + jnp.tanh(w_lo) @ w2)) - 0.5
    a = jax.nn.sigmoid(a0 + a_lo @ a2)
    g = jax.nn.sigmoid(g_lo) @ g2
    kk = heads(k * k_k)
    kk = kk / jnp.maximum(jnp.linalg.norm(kk, axis=-1, keepdims=True), 1e-12)
    k = k * (1.0 + (a - 1.0) * k_a)
    decay = jnp.exp(-jnp.exp(heads(w)))
    rh, kh, vh, ah = heads(r), heads(k), heads(v), heads(a)
    y = _rwkv7_recurrence(rh, decay, kh, vh, -kk, kk * ah)
    y = _head_layernorm(y, A_GN_EPS) * ln_w.reshape(A_HEADS, HEAD_DIM) + ln_b.reshape(A_HEADS, HEAD_DIM)
    y = y + jnp.sum(rh * kh * r_k, axis=-1, keepdims=True) * vh
    return y.reshape(Bsz, T, A_WIDTH).astype(dt) * g


def dsa_mixer(q, c_kv, iq, ik, iw, kv_norm, w_uk, w_uv):
    Bsz, T, _ = q.shape
    n_keys = T
    topk = min(B_TOPK_MAX, n_keys // 4)
    nb = T // B_QUERY_BLOCK
    c = rmsnorm(c_kv, kv_norm)
    qh = q.reshape(Bsz, T, B_HEADS, HEAD_DIM)
    q_lat = jnp.einsum('bthd,hdc->bthc', qh, w_uk) * HEAD_DIM ** -0.5
    iqh = iq.reshape(Bsz, T, B_IDX_HEADS, B_IDX_DIM) * B_IDX_DIM ** -0.5
    iwh = iw * B_IDX_HEADS ** -0.5
    key_pos = jnp.arange(n_keys)
    blocks = lambda z: jnp.moveaxis(z.reshape((Bsz, nb, B_QUERY_BLOCK) + z.shape[2:]), 1, 0)

    def one_block(args):
        q_lat_b, iq_b, iw_b, blk = args
        q_pos = blk * B_QUERY_BLOCK + jnp.arange(B_QUERY_BLOCK)
        rel = jax.nn.relu(jnp.einsum('bqhd,bsd->bqhs', iq_b, ik))
        score = jnp.einsum('bqh,bqhs->bqs', iw_b, rel).astype(jnp.float32)
        visible = key_pos[None, :] <= q_pos[:, None]
        score = jnp.where(visible[None], score, -jnp.inf)
        _, idx = lax.top_k(score, topk)
        valid = idx <= q_pos[None, :, None]
        c_sel = jax.vmap(lambda cb, ib: cb[ib])(c, idx)
        logits = jnp.einsum('bqhc,bqkc->bqhk', q_lat_b, c_sel).astype(jnp.float32)
        logits = jnp.where(valid[:, :, None, :], logits, -jnp.inf)
        prob = jax.nn.softmax(logits, axis=-1).astype(c.dtype)
        o_lat = jnp.einsum('bqhk,bqkc->bqhc', prob, c_sel)
        return jnp.einsum('bqhc,hcd->bqhd', o_lat, w_uv)

    out = lax.map(one_block, (blocks(q_lat), blocks(iqh), blocks(iwh), jnp.arange(nb)))
    return jnp.moveaxis(out, 0, 1).reshape(Bsz, T, B_WIDTH)


def _retnet_rotate(z):
    T = z.shape[1]
    half = HEAD_DIM // 2
    theta = C_ROPE_BASE ** (-jnp.linspace(0.0, 1.0, half, dtype=jnp.float32))
    ang = jnp.arange(T, dtype=jnp.float32)[:, None] * theta[None, :]
    cos, sin = jnp.cos(ang)[None, :, None, :], jnp.sin(ang)[None, :, None, :]
    z1, z2 = z[..., :half], z[..., half:]
    return jnp.concatenate([z1 * cos - z2 * sin, z1 * sin + z2 * cos], axis=-1)


def retention_mixer(q, k, v, g):
    Bsz, T, _ = q.shape
    nc = T // C_CHUNK
    dt = q.dtype
    f32 = jnp.float32
    heads = lambda z: z.astype(f32).reshape(Bsz, T, C_HEADS, HEAD_DIM)
    qh = _retnet_rotate(heads(q))
    kh = _retnet_rotate(heads(k)) * HEAD_DIM ** -0.5
    vh = heads(v)
    log_gamma = jnp.log(1.0 - 2.0 ** (-5.0 - jnp.arange(C_HEADS, dtype=f32)))
    n = jnp.arange(C_CHUNK, dtype=f32)
    diff = n[:, None] - n[None, :]
    intra = jnp.where(diff[None] >= 0, jnp.exp(jnp.maximum(diff, 0.0)[None] * log_gamma[:, None, None]), 0.0)
    q_decay = jnp.exp((n[:, None] + 1.0) * log_gamma[None, :])
    k_decay = jnp.exp((C_CHUNK - 1.0 - n)[:, None] * log_gamma[None, :])
    chunk_decay = jnp.exp(C_CHUNK * log_gamma)
    chunks = lambda z: jnp.moveaxis(z.reshape(Bsz, nc, C_CHUNK, C_HEADS, HEAD_DIM), 1, 0)

    def step(state, inp):
        qc, kc, vc = inp
        scores = jnp.einsum('bnhd,bmhd->bhnm', qc, kc) * intra[None]
        inner = jnp.einsum('bhnm,bmhe->bnhe', scores, vc)
        cross = jnp.einsum('bnhd,bhde->bnhe', qc, state) * q_decay[None, :, :, None]
        state = state * chunk_decay[None, :, None, None] + jnp.einsum('bmhd,bmhe->bhde', kc * k_decay[None, :, :, None], vc)
        return state, inner + cross

    state0 = jnp.zeros((Bsz, C_HEADS, HEAD_DIM, HEAD_DIM), f32)
    _, y = lax.scan(step, state0, (chunks(qh), chunks(kh), chunks(vh)))
    y = jnp.moveaxis(y, 0, 1).reshape(Bsz, T, C_HEADS, HEAD_DIM)
    y = y * lax.rsqrt(jnp.mean(y * y, axis=-1, keepdims=True) + NORM_EPS)
    return jax.nn.silu(g) * y.reshape(Bsz, T, C_WIDTH).astype(dt)


def setup_inputs(seed: int = 0) -> dict:
    key = jax.random.key(seed)
    counter = [0]

    def nk():
        counter[0] += 1
        return jax.random.fold_in(key, counter[0])

    def nrm(shape, scale):
        return jax.random.normal(nk(), shape, jnp.float32) * scale

    def gain(shape):
        return 1.0 + nrm(shape, 0.02)

    L = DEPTH
    return {
        'x': nrm((BATCH, SEQ, D_MODEL), 1.0),
        'ffn1_norm': gain((L, D_MODEL)),
        'ffn1_w_gate': nrm((L, D_MODEL, D_FF), D_MODEL ** -0.5),
        'ffn1_w_up': nrm((L, D_MODEL, D_FF), D_MODEL ** -0.5),
        'ffn1_w_down': nrm((L, D_FF, D_MODEL), D_FF ** -0.5),
        'mix_norm': gain((L, D_MODEL)),
        'w_in': nrm((L, D_MODEL, N_IN), D_MODEL ** -0.5),
        'w_out': nrm((L, D_MIX, D_MODEL), D_MIX ** -0.5),
        'rwkv_mu': jax.random.uniform(nk(), (L, A_IN), jnp.float32),
        'rwkv_w0': jnp.linspace(-6.0, -1.0, A_WIDTH, dtype=jnp.float32)[None, :] + nrm((L, A_WIDTH), 0.1),
        'rwkv_w2': nrm((L, A_DECAY_LORA, A_WIDTH), 0.1),
        'rwkv_a0': nrm((L, A_WIDTH), 0.1),
        'rwkv_a2': nrm((L, A_AAA_LORA, A_WIDTH), 0.1),
        'rwkv_g2': nrm((L, A_GATE_LORA, A_WIDTH), A_GATE_LORA ** -0.5),
        'rwkv_k_k': 0.85 + nrm((L, A_WIDTH), 0.02),
        'rwkv_k_a': 1.0 + nrm((L, A_WIDTH), 0.02),
        'rwkv_r_k': nrm((L, A_HEADS, HEAD_DIM), 0.1),
        'rwkv_ln_w': gain((L, A_WIDTH)),
        'rwkv_ln_b': nrm((L, A_WIDTH), 0.01),
        'rwkv_vres_w_in': nrm((L - 1, D_MODEL, A_MV_LORA), D_MODEL ** -0.5),
        'rwkv_vres_mu': jax.random.uniform(nk(), (L - 1, A_MV_LORA), jnp.float32),
        'rwkv_v0': 0.5 + nrm((L - 1, A_WIDTH), 0.1),
        'rwkv_v2': nrm((L - 1, A_MV_LORA, A_WIDTH), 0.1),
        'dsa_kv_norm': gain((L, B_KV_LATENT)),
        'dsa_w_uk': nrm((L, B_HEADS, HEAD_DIM, B_KV_LATENT), B_KV_LATENT ** -0.5),
        'dsa_w_uv': nrm((L, B_HEADS, B_KV_LATENT, HEAD_DIM), B_KV_LATENT ** -0.5),
        'ffn2_norm': gain((L, D_MODEL)),
        'ffn2_w_gate': nrm((L, D_MODEL, D_FF), D_MODEL ** -0.5),
        'ffn2_w_up': nrm((L, D_MODEL, D_FF), D_MODEL ** -0.5),
        'ffn2_w_down': nrm((L, D_FF, D_MODEL), D_FF ** -0.5),
        'final_norm': gain((D_MODEL,)),
    }


def reference(x, ffn1_norm, ffn1_w_gate, ffn1_w_up, ffn1_w_down, mix_norm, w_in, w_out,
              rwkv_mu, rwkv_w0, rwkv_w2, rwkv_a0, rwkv_a2, rwkv_g2, rwkv_k_k, rwkv_k_a, rwkv_r_k,
              rwkv_ln_w, rwkv_ln_b, rwkv_vres_w_in, rwkv_vres_mu, rwkv_v0, rwkv_v2,
              dsa_kv_norm, dsa_w_uk, dsa_w_uv,
              ffn2_norm, ffn2_w_gate, ffn2_w_up, ffn2_w_down, final_norm):
    v_first = None
    for l in range(DEPTH):
        x = x + 0.5 * swiglu(rmsnorm(x, ffn1_norm[l]), ffn1_w_gate[l], ffn1_w_up[l], ffn1_w_down[l])
        h = rmsnorm(x, mix_norm[l])
        if l == 0:
            p_a, p_b, p_c = _split(h @ w_in[l], (A_IN, B_IN, C_IN))
        else:
            w_comb = jnp.concatenate([w_in[l], rwkv_vres_w_in[l - 1]], axis=1)
            p_a, p_b, p_c, p_mv = _split(h @ w_comb, (A_IN, B_IN, C_IN, A_MV_LORA))
        r, k, v, w_lo, a_lo, g_lo = _split(shift_lerp(p_a, rwkv_mu[l]), A_SLAB)
        if l == 0:
            v_first = v
        else:
            mv = shift_lerp(p_mv, rwkv_vres_mu[l - 1])
            v = v + (v_first - v) * jax.nn.sigmoid(rwkv_v0[l - 1] + mv @ rwkv_v2[l - 1])
        o_a = rwkv7_mixer(r, k, v, w_lo, a_lo, g_lo, rwkv_w0[l], rwkv_w2[l], rwkv_a0[l], rwkv_a2[l],
                          rwkv_g2[l], rwkv_k_k[l], rwkv_k_a[l], rwkv_r_k[l], rwkv_ln_w[l], rwkv_ln_b[l])
        q_b, c_b, iq_b, ik_b, iw_b = _split(p_b, B_SLAB)
        o_b = dsa_mixer(q_b, c_b, iq_b, ik_b, iw_b, dsa_kv_norm[l], dsa_w_uk[l], dsa_w_uv[l])
        q_c, k_c, v_c, g_c = _split(p_c, C_SLAB)
        o_c = retention_mixer(q_c, k_c, v_c, g_c)
        x = x + jnp.concatenate([o_a, o_b, o_c], axis=-1) @ w_out[l]
        x = x + 0.5 * swiglu(rmsnorm(x, ffn2_norm[l]), ffn2_w_gate[l], ffn2_w_up[l], ffn2_w_down[l])
    return rmsnorm(x, final_norm)
```

```python
import functools
import math

import jax
import jax.numpy as jnp
from jax import lax
from jax.experimental import pallas as pl
from jax.experimental.pallas import tpu as pltpu

F32 = jnp.float32
BF16 = jnp.bfloat16
I32 = jnp.int32

LANES = 128
HEAD_DIM = 64
PAIR = 2 * HEAD_DIM
CHUNK = 64
NORM_EPS = 1e-6
A_GN_EPS = 64e-5
D_MODEL = 1024
D_FF = 2752
D_FF_PAD = 2816
FF_TILE = 256
A_WIDTH = 384
B_WIDTH = 384
C_WIDTH = 256
KV_LATENT = 128
IDX_HEADS = 8
B_HEADS = 6
TOPK_MAX = 256
QBLK = 128
ROPE_BASE = 10000.0
NEG_BIG = -1e30
INT_MIN = -2147483648
VMEM_LIMIT = 52 * 1024 * 1024


def _cp(sem):
    return pltpu.CompilerParams(dimension_semantics=sem, vmem_limit_bytes=VMEM_LIMIT)


def _dot(a, b):
    return jnp.dot(a.astype(BF16), b.astype(BF16), preferred_element_type=F32)


def _dot_nt(a, b):
    return lax.dot_general(a.astype(BF16), b.astype(BF16), (((1,), (1,)), ((), ())),
                           preferred_element_type=F32)


def _dot_tn(a, b):
    return lax.dot_general(a.astype(BF16), b.astype(BF16), (((0,), (0,)), ((), ())),
                           preferred_element_type=F32)


def _dot_hl(m01, x):
    hi = x.astype(BF16)
    lo = (x - hi.astype(F32)).astype(BF16)
    return (jnp.dot(m01, hi, preferred_element_type=F32)
            + jnp.dot(m01, lo, preferred_element_type=F32))


def _dot_hl_r(x, m01):
    hi = x.astype(BF16)
    lo = (x - hi.astype(F32)).astype(BF16)
    return (jnp.dot(hi, m01, preferred_element_type=F32)
            + jnp.dot(lo, m01, preferred_element_type=F32))


def _rms(x, g):
    return x * lax.rsqrt(jnp.mean(x * x, axis=-1, keepdims=True) + NORM_EPS) * g


def _sigmoid(x):
    return 1.0 / (1.0 + jnp.exp(-x))


def _ffn_kernel(*refs, has_mix, final):
    it = iter(refs)
    x_ref = next(it)
    if has_mix:
        oa_ref, ob_ref, oc_ref, woa_ref, wob_ref, woc_ref = (next(it) for _ in range(6))
    g_ref, wg_ref, wu_ref, wd_ref = (next(it) for _ in range(4))
    fg_ref = next(it) if final else None
    out_ref = next(it)
    xs_ref = next(it)
    h_ref = next(it)
    j = pl.program_id(1)

    @pl.when(j == 0)
    def _():
        x = x_ref[...]
        if has_mix:
            x = (x + _dot(oa_ref[...], woa_ref[...]) + _dot(ob_ref[...], wob_ref[...])
                 + _dot(oc_ref[...], woc_ref[...]))
        xs_ref[...] = x
        h_ref[...] = _rms(x, g_ref[...]).astype(BF16)
        out_ref[...] = jnp.zeros_like(out_ref)

    h = h_ref[...]
    gate = jnp.dot(h, wg_ref[...], preferred_element_type=F32)
    up = jnp.dot(h, wu_ref[...], preferred_element_type=F32)
    act = (gate * _sigmoid(gate) * up).astype(BF16)
    out_ref[...] += jnp.dot(act, wd_ref[...], preferred_element_type=F32)

    @pl.when(j == pl.num_programs(1) - 1)
    def _():
        y = xs_ref[...] + 0.5 * out_ref[...]
        if final:
            y = _rms(y, fg_ref[...])
        out_ref[...] = y


def _ffn_call(x, norm_g, wg, wu, wd, mix=None, final_g=None):
    n = x.shape[0]
    tm = min(1024, n)
    nj = D_FF_PAD // FF_TILE
    row = lambda i, j: (i, 0)
    const = lambda i, j: (0, 0)
    in_specs = [pl.BlockSpec((tm, D_MODEL), row)]
    args = [x]
    if mix is not None:
        oa, ob, oc, woa, wob, woc = mix
        in_specs += [pl.BlockSpec((tm, A_WIDTH), row), pl.BlockSpec((tm, B_WIDTH), row),
                     pl.BlockSpec((tm, C_WIDTH), row),
                     pl.BlockSpec((A_WIDTH, D_MODEL), const), pl.BlockSpec((B_WIDTH, D_MODEL), const),
                     pl.BlockSpec((C_WIDTH, D_MODEL), const)]
        args += [oa, ob, oc, woa, wob, woc]
    in_specs += [pl.BlockSpec((1, D_MODEL), const),
                 pl.BlockSpec((D_MODEL, FF_TILE), lambda i, j: (0, j)),
                 pl.BlockSpec((D_MODEL, FF_TILE), lambda i, j: (0, j)),
                 pl.BlockSpec((FF_TILE, D_MODEL), lambda i, j: (j, 0))]
    args += [norm_g, wg, wu, wd]
    if final_g is not None:
        in_specs.append(pl.BlockSpec((1, D_MODEL), const))
        args.append(final_g)
    return pl.pallas_call(
        functools.partial(_ffn_kernel, has_mix=mix is not None, final=final_g is not None),
        grid=(n // tm, nj),
        in_specs=in_specs,
        out_specs=pl.BlockSpec((tm, D_MODEL), row),
        out_shape=jax.ShapeDtypeStruct((n, D_MODEL), F32),
        scratch_shapes=[pltpu.VMEM((tm, D_MODEL), F32), pltpu.VMEM((tm, D_MODEL), BF16)],
        compiler_params=_cp(("parallel", "arbitrary")),
        name="ffn",
    )(*args)


MIX_SLABS = (("rkv", 1152), ("lm", 256), ("bq", 384), ("iq", 512), ("misc", 384), ("pc", 1024))
MIX_COLS = sum(w for _, w in MIX_SLABS)


def _mix_in_kernel(x_ref, g_ref, w_ref, *out_refs):
    h = _rms(x_ref[...], g_ref[...]).astype(BF16)
    off = 0
    for ref, (_, width) in zip(out_refs, MIX_SLABS):
        step = 384 if width % 384 == 0 else 256
        for a in range(0, width, step):
            ref[:, a:a + step] = jnp.dot(h, w_ref[:, off + a:off + a + step],
                                         preferred_element_type=F32)
        off += width


def _mix_in_call(x, norm_g, w_packed):
    n = x.shape[0]
    tm = min(512, n)
    row = lambda i: (i, 0)
    const = lambda i: (0, 0)
    return pl.pallas_call(
        _mix_in_kernel,
        grid=(n // tm,),
        in_specs=[pl.BlockSpec((tm, D_MODEL), row), pl.BlockSpec((1, D_MODEL), const),
                  pl.BlockSpec((D_MODEL, MIX_COLS), const)],
        out_specs=[pl.BlockSpec((tm, w), row) for _, w in MIX_SLABS],
        out_shape=[jax.ShapeDtypeStruct((n, w), F32) for _, w in MIX_SLABS],
        compiler_params=_cp(("parallel",)),
        name="mix_in",
    )(x, norm_g, w_packed)


def _shift_lerp(p, mu, carry_ref, first_tile):
    rows = lax.broadcasted_iota(I32, p.shape, 0)
    carry = jnp.where(first_tile, jnp.zeros_like(carry_ref[0:1, :]), carry_ref[0:1, :])
    prev = jnp.where(rows == 0, carry, pltpu.roll(p, 1, axis=0))
    carry_ref[0:1, :] = p[p.shape[0] - 1:, :]
    return p + (prev - p) * mu


def _rwkv_prep_kernel(*refs, has_vres):
    it = iter(refs)
    rkv_ref, lm_ref = next(it), next(it)
    vfirst_ref = next(it) if has_vres else None
    (mu_rkv_ref, mu_lm_ref, w0_ref, a0_ref, wl_ref, kk_ref, ka_ref, v0_ref, v2_ref,
     ones_ref) = (next(it) for _ in range(10))
    r_ref, kp_ref, v_ref, nk_ref, beta_ref, lw_ref, g_ref = (next(it) for _ in range(7))
    c_rkv_ref, c_lm_ref = next(it), next(it)

    first = pl.program_id(1) == 0
    xa = _shift_lerp(rkv_ref[...], mu_rkv_ref[...], c_rkv_ref, first)
    xl = _shift_lerp(lm_ref[...], mu_lm_ref[...], c_lm_ref, first)
    r = xa[:, 0:A_WIDTH]
    k = xa[:, A_WIDTH:2 * A_WIDTH]
    v = xa[:, 2 * A_WIDTH:3 * A_WIDTH]
    lora = xl[:, 0:LANES]
    lane = lax.broadcasted_iota(I32, lora.shape, 1)
    feat = jnp.where(lane < 32, jnp.tanh(lora), jnp.where(lane < 64, lora, _sigmoid(lora)))
    proj = _dot(feat, wl_ref[...])
    w_pre = w0_ref[...] + proj[:, 0:A_WIDTH]
    sp = jnp.maximum(-w_pre, 0.0) + jnp.log(1.0 + jnp.exp(-jnp.abs(w_pre)))
    lw_ref[...] = -jnp.exp(-sp - 0.5)
    eta = _sigmoid(a0_ref[...] + proj[:, A_WIDTH:2 * A_WIDTH])
    g_ref[...] = proj[:, 2 * A_WIDTH:3 * A_WIDTH]
    kk = k * kk_ref[...]
    ss = _dot_hl_r(kk * kk, ones_ref[...])
    nk = kk / jnp.maximum(jnp.sqrt(ss), 1e-12)
    nk_ref[...] = nk
    beta_ref[...] = nk * eta
    kp_ref[...] = k * (1.0 + (eta - 1.0) * ka_ref[...])
    r_ref[...] = r
    if has_vres:
        mv = xl[:, LANES:2 * LANES]
        mix = _sigmoid(v0_ref[...] + _dot(mv, v2_ref[...]))
        v = v + (vfirst_ref[...] - v) * mix
    v_ref[...] = v


def _rwkv_prep_call(rkv, lm, vfirst, pw, bsz, t):
    tt = min(256, t)
    nt = t // tt
    tok = lambda b, i: (b * nt + i, 0)
    const = lambda b, i: (0, 0)
    has_vres = vfirst is not None
    in_specs = [pl.BlockSpec((tt, 1152), tok), pl.BlockSpec((tt, 256), tok)]
    args = [rkv, lm]
    if has_vres:
        in_specs.append(pl.BlockSpec((tt, A_WIDTH), tok))
        args.append(vfirst)
    params = [pw["mu_rkv"], pw["mu_lm"], pw["w0"], pw["a0"], pw["w_lora"], pw["k_k"], pw["k_a"],
              pw["v0"], pw["v2"], pw["ones_a"]]
    in_specs += [pl.BlockSpec(p.shape, const) for p in params]
    args += params
    n = bsz * t
    out = jax.ShapeDtypeStruct((n, A_WIDTH), F32)
    return pl.pallas_call(
        functools.partial(_rwkv_prep_kernel, has_vres=has_vres),
        grid=(bsz, nt),
        in_specs=in_specs,
        out_specs=[pl.BlockSpec((tt, A_WIDTH), tok)] * 7,
        out_shape=[out] * 7,
        scratch_shapes=[pltpu.VMEM((8, 1152), F32), pltpu.VMEM((8, 256), F32)],
        compiler_params=_cp(("parallel", "arbitrary")),
        name="rwkv_prep",
    )(*args)


def _bd(x, bd_mask):
    return jnp.where(bd_mask, jnp.concatenate([x, x], axis=0), 0.0)


def _chunk_masks():
    r = lax.broadcasted_iota(I32, (PAIR, PAIR), 0)
    c = lax.broadcasted_iota(I32, (PAIR, PAIR), 1)
    same = (r >> 6) == (c >> 6)
    return r, c, same


def _rwkv_core_kernel(r_ref, kp_ref, v_ref, nk_ref, beta_ref, lw_ref, g_ref,
                      lnw_ref, lnb_ref, rk_ref, tri_ref, lvl_ref, ones_ref,
                      o_ref, s_ref):
    @pl.when(pl.program_id(1) == 0)
    def _():
        s_ref[...] = jnp.zeros_like(s_ref)

    row, col, same = _chunk_masks()
    rl, cl = row & (CHUNK - 1), col & (CHUNK - 1)
    strict = same & (rl > cl)
    incl = same & (rl >= cl)
    eye = (row == col).astype(F32)
    tri = tri_ref[...]
    ones_pair = ones_ref[...]
    n_chunks = r_ref.shape[0] // CHUNK

    def chunk_body(c, carry):
        rows = pl.ds(pl.multiple_of(c * CHUNK, CHUNK), CHUNK)
        for p in range(A_WIDTH // PAIR):
            sl = slice(p * PAIR, (p + 1) * PAIR)
            r = r_ref[rows, sl]
            kp = kp_ref[rows, sl]
            v = v_ref[rows, sl]
            nk = nk_ref[rows, sl]
            beta = beta_ref[rows, sl]
            lw = lw_ref[rows, sl]
            cum = _dot_hl(tri, lw)
            cum_l = cum[CHUNK - 1:CHUNK, :]
            g_in = jnp.exp(cum)
            g_ex = jnp.exp(cum - lw)
            g_inv = jnp.exp(-cum)
            g_tail = jnp.exp(cum_l - cum)
            a_bd = _bd(-nk * g_ex, same)
            r_bd = _bd(r * g_in, same)
            b_bd = _bd(beta * g_inv, same)
            k_bd = _bd(kp * g_inv, same)
            v_bd = _bd(v, same)
            bg_bd = _bd(beta * g_tail, same)
            kg_bd = _bd(kp * g_tail, same)

            ar = jnp.concatenate([a_bd, r_bd], axis=0)
            g1 = _dot_nt(ar, b_bd)
            g2 = _dot_nt(ar, k_bd)
            a_ab = jnp.where(strict, g1[:PAIR], 0.0)
            a_rb = jnp.where(incl, g1[PAIR:], 0.0)
            a_ak = jnp.where(strict, g2[:PAIR], 0.0)
            a_rk = jnp.where(incl, g2[PAIR:], 0.0)

            t_inv = eye + a_ab * lvl_ref[0]
            for lvl in range(1, 6):
                t_inv = t_inv + _dot(t_inv, _dot(a_ab * lvl_ref[lvl], t_inv))

            w_loc = _dot(a_ak, v_bd)
            au = _dot(t_inv, jnp.concatenate([a_bd, w_loc], axis=1))
            ry = (jnp.concatenate([r_bd, _dot(a_rk, v_bd)], axis=1) + _dot(a_rb, au))
            mn = _dot_tn(au, bg_bd)
            m_t = mn[:PAIR]
            n_t = mn[PAIR:] + _dot_tn(v_bd, kg_bd)

            s = s_ref[p]
            y_bd = _dot_nt(ry[:, :PAIR], s) + ry[:, PAIR:]
            s_ref[p] = s * jnp.exp(cum_l) + _dot(s, m_t) + n_t
            y = y_bd[:CHUNK] + y_bd[CHUNK:]

            mean = _dot_hl_r(y, ones_pair) * (1.0 / HEAD_DIM)
            yc = y - mean
            var = _dot_hl_r(yc * yc, ones_pair) * (1.0 / HEAD_DIM)
            yn = yc * lax.rsqrt(var + A_GN_EPS) * lnw_ref[:, sl] + lnb_ref[:, sl]
            bonus = _dot_hl_r(r * kp * rk_ref[:, sl], ones_pair)
            o_ref[rows, sl] = ((yn + bonus * v) * g_ref[rows, sl]).astype(o_ref.dtype)
        return carry

    lax.fori_loop(0, n_chunks, chunk_body, 0)


def _rwkv_core_call(coef, pw, consts, bsz, t):
    tt = min(512, t)
    nt = t // tt
    tok = lambda b, i: (b * nt + i, 0)
    c2 = lambda b, i: (0, 0)
    c3 = lambda b, i: (0, 0, 0)
    params = [pw["ln_w"], pw["ln_b"], pw["r_k"]]
    return pl.pallas_call(
        _rwkv_core_kernel,
        grid=(bsz, nt),
        in_specs=([pl.BlockSpec((tt, A_WIDTH), tok)] * 7
                  + [pl.BlockSpec((1, A_WIDTH), c2)] * 3
                  + [pl.BlockSpec((CHUNK, CHUNK), c2), pl.BlockSpec((6, PAIR, PAIR), c3),
                     pl.BlockSpec((PAIR, PAIR), c2)]),
        out_specs=pl.BlockSpec((tt, A_WIDTH), tok),
        out_shape=jax.ShapeDtypeStruct((bsz * t, A_WIDTH), BF16),
        scratch_shapes=[pltpu.VMEM((A_WIDTH // PAIR, PAIR, PAIR), F32)],
        compiler_params=_cp(("parallel", "arbitrary")),
        name="rwkv_core",
    )(*coef, *params, consts["tri"], consts["levels"], consts["ones_pair"])


def _swap_halves(z):
    lane = lax.broadcasted_iota(I32, z.shape, 1)
    return jnp.where((lane & (HEAD_DIM // 2)) == 0,
                     pltpu.roll(z, LANES - HEAD_DIM // 2, axis=1),
                     pltpu.roll(z, HEAD_DIM // 2, axis=1))


def _ret_kernel(pc_ref, cos_ref, sin_ref, dmask_ref, qdec_ref, kdec_ref, cdec_ref, ones_ref,
                o_ref, z_ref):
    @pl.when(pl.program_id(1) == 0)
    def _():
        z_ref[...] = jnp.zeros_like(z_ref)

    _, _, same = _chunk_masks()
    ones_pair = ones_ref[...]
    n_chunks = pc_ref.shape[0] // CHUNK

    def chunk_body(c, carry):
        rows = pl.ds(pl.multiple_of(c * CHUNK, CHUNK), CHUNK)
        for p in range(C_WIDTH // PAIR):
            sl = slice(p * PAIR, (p + 1) * PAIR)
            cos = cos_ref[rows, sl]
            sin = sin_ref[rows, sl]
            q = pc_ref[rows, p * PAIR:(p + 1) * PAIR]
            k = pc_ref[rows, C_WIDTH + p * PAIR:C_WIDTH + (p + 1) * PAIR]
            v = pc_ref[rows, 2 * C_WIDTH + p * PAIR:2 * C_WIDTH + (p + 1) * PAIR]
            g = pc_ref[rows, 3 * C_WIDTH + p * PAIR:3 * C_WIDTH + (p + 1) * PAIR]
            qr = q * cos + _swap_halves(q) * sin
            kr = (k * cos + _swap_halves(k) * sin) * (HEAD_DIM ** -0.5)
            q_bd = _bd(qr, same)
            k_bd = _bd(kr, same)
            v_bd = _bd(v, same)
            scores = _dot_nt(q_bd, k_bd) * dmask_ref[p]
            z = z_ref[p]
            y_bd = _dot(scores, v_bd) + _dot(_bd(qr * qdec_ref[:, sl], same), z)
            z_ref[p] = z * cdec_ref[:, sl] + _dot_tn(_bd(kr * kdec_ref[:, sl], same), v_bd)
            y = y_bd[:CHUNK] + y_bd[CHUNK:]
            ms = _dot_hl_r(y * y, ones_pair) * (1.0 / HEAD_DIM)
            y = y * lax.rsqrt(ms + NORM_EPS)
            o_ref[rows, sl] = (g * _sigmoid(g) * y).astype(o_ref.dtype)
        return carry

    lax.fori_loop(0, n_chunks, chunk_body, 0)


def _ret_call(pc, consts, bsz, t):
    tt = min(512, t)
    nt = t // tt
    tok = lambda b, i: (b * nt + i, 0)
    c2 = lambda b, i: (0, 0)
    return pl.pallas_call(
        _ret_kernel,
        grid=(bsz, nt),
        in_specs=[pl.BlockSpec((tt, 4 * C_WIDTH), tok),
                  pl.BlockSpec((tt, C_WIDTH), lambda b, i: (i, 0)),
                  pl.BlockSpec((tt, C_WIDTH), lambda b, i: (i, 0)),
                  pl.BlockSpec((C_WIDTH // PAIR, PAIR, PAIR), lambda b, i: (0, 0, 0)),
                  pl.BlockSpec((CHUNK, C_WIDTH), c2), pl.BlockSpec((CHUNK, C_WIDTH), c2),
                  pl.BlockSpec((1, C_WIDTH), c2), pl.BlockSpec((PAIR, PAIR), c2)],
        out_specs=pl.BlockSpec((tt, C_WIDTH), tok),
        out_shape=jax.ShapeDtypeStruct((bsz * t, C_WIDTH), BF16),
        scratch_shapes=[pltpu.VMEM((C_WIDTH // PAIR, PAIR, PAIR), F32)],
        compiler_params=_cp(("parallel", "arbitrary")),
        name="retention",
    )(pc, consts["cos"], consts["sin"], consts["ret_dmask"], consts["ret_qdec"],
      consts["ret_kdec"], consts["ret_cdec"], consts["ones_pair"])


def _dsa_kernel(bq_ref, iq_ref, misc_ref, kvn_ref, wuk_ref, wuvt_ref, tri_ref,
                o_ref, c_ref, ct_ref, key_ref, logit_ref, rhs_ref, qls_ref, acc_ref, *, topk):
    j = pl.program_id(1)
    t = misc_ref.shape[0]
    nkb = j + 1
    n_heads = B_HEADS

    @pl.when(j == 0)
    def _():
        def body(i, carry):
            rows = pl.ds(pl.multiple_of(i * QBLK, QBLK), QBLK)
            c = _rms(misc_ref[rows, 0:KV_LATENT], kvn_ref[...])
            c_ref[rows, :] = c.astype(BF16)
            ct_ref[:, rows] = c.T.astype(BF16)
            return carry
        lax.fori_loop(0, t // QBLK, body, 0)

    qrows = pl.ds(pl.multiple_of(j * QBLK, QBLK), QBLK)
    qlat = _dot(bq_ref[...], wuk_ref[...]) * (HEAD_DIM ** -0.5)
    for h in range(n_heads):
        qls_ref[h * QBLK:(h + 1) * QBLK, :] = qlat[:, h * LANES:(h + 1) * LANES].astype(BF16)
    iq = iq_ref[...] * (HEAD_DIM ** -0.5)
    lane = lax.broadcasted_iota(I32, (QBLK, LANES), 1)
    for hp in range(IDX_HEADS // 2):
        slab = iq[:, hp * LANES:(hp + 1) * LANES]
        rhs_ref[hp, 0:QBLK, :] = jnp.where(lane < HEAD_DIM, slab, 0.0).astype(BF16)
        rhs_ref[hp, QBLK:2 * QBLK, :] = jnp.where(lane >= HEAD_DIM, slab, 0.0).astype(BF16)
    iw_t = misc_ref[qrows, 2 * LANES:3 * LANES].T * (IDX_HEADS ** -0.5)

    srow = lax.broadcasted_iota(I32, (QBLK, QBLK), 0)
    qcol = lax.broadcasted_iota(I32, (QBLK, QBLK), 1)
    qpos = j * QBLK + qcol

    def visible(kb):
        return (kb * QBLK + srow) <= qpos

    def score_body(kb, carry):
        rows = pl.ds(pl.multiple_of(kb * QBLK, QBLK), QBLK)
        ik2 = misc_ref[rows, LANES:2 * LANES].astype(BF16)
        acc = jnp.zeros((QBLK, QBLK), F32)
        for hp in range(IDX_HEADS // 2):
            rel = lax.dot_general(ik2, rhs_ref[hp], (((1,), (1,)), ((), ())),
                                  preferred_element_type=F32)
            acc = acc + iw_t[2 * hp:2 * hp + 1, :] * jnp.maximum(rel[:, :QBLK], 0.0)
            acc = acc + iw_t[2 * hp + 1:2 * hp + 2, :] * jnp.maximum(rel[:, QBLK:], 0.0)
        score = jnp.where(visible(kb), acc + 0.0, -jnp.inf)
        bits = pltpu.bitcast(score, I32)
        key_ref[rows, :] = jnp.where(bits < 0, bits ^ 0x7FFFFFFF, bits)
        return carry
    lax.fori_loop(0, nkb, score_body, 0)

    def count(pred):
        def body(kb, acc):
            rows = pl.ds(pl.multiple_of(kb * QBLK, QBLK), QBLK)
            return acc + jnp.where(pred(key_ref[rows, :]), 1, 0)
        acc = lax.fori_loop(0, nkb, body, jnp.zeros((QBLK, QBLK), I32))
        return jnp.sum(acc, axis=0, keepdims=True)

    c0 = count(lambda k: k >= 0)
    tau0 = jnp.where(c0 >= topk, 0, INT_MIN).astype(I32)

    def bit_body(i, tau):
        cand = tau | jnp.left_shift(jnp.int32(1), 30 - i)
        cnt = count(lambda k: k >= cand)
        return jnp.where(cnt >= topk, cand, tau)
    tau = lax.fori_loop(0, 31, bit_body, tau0)
    need = topk - count(lambda k: k > tau)

    tri = tri_ref[...]

    def logit_body(kb, carry):
        seen, m = carry
        rows = pl.ds(pl.multiple_of(kb * QBLK, QBLK), QBLK)
        key = key_ref[rows, :]
        eq = key == tau
        rank = seen + jnp.dot(tri, jnp.where(eq, 1.0, 0.0).astype(BF16),
                              preferred_element_type=F32)
        sel = visible(kb) & ((key > tau) | (eq & (rank <= need.astype(F32))))
        logits = lax.dot_general(c_ref[rows, :], qls_ref[...], (((1,), (1,)), ((), ())),
                                 preferred_element_type=F32)
        new_m = []
        for h in range(n_heads):
            lh = jnp.where(sel, logits[:, h * QBLK:(h + 1) * QBLK], NEG_BIG)
            logit_ref[rows, h * QBLK:(h + 1) * QBLK] = lh
            new_m.append(jnp.max(lh, axis=0, keepdims=True))
        return rank[QBLK - 1:QBLK, :], jnp.maximum(m, jnp.concatenate(new_m, axis=1))

    _, m = lax.fori_loop(0, nkb, logit_body,
                         (jnp.zeros((1, QBLK), F32), jnp.full((1, n_heads * QBLK), NEG_BIG, F32)))

    acc_ref[...] = jnp.zeros_like(acc_ref)

    def pv_body(kb, lsum):
        rows = pl.ds(pl.multiple_of(kb * QBLK, QBLK), QBLK)
        p = jnp.exp(logit_ref[rows, :] - m)
        acc_ref[...] += jnp.dot(ct_ref[:, rows], p.astype(BF16), preferred_element_type=F32)
        return lsum + jnp.sum(p, axis=0, keepdims=True)
    lsum = lax.fori_loop(0, nkb, pv_body, jnp.zeros((1, n_heads * QBLK), F32))

    o_lat = acc_ref[...] / lsum
    outs = [jnp.dot(wuvt_ref[h], o_lat[:, h * QBLK:(h + 1) * QBLK].astype(BF16),
                    preferred_element_type=F32) for h in range(n_heads)]
    o_ref[...] = jnp.concatenate(outs, axis=0).T.astype(o_ref.dtype)


def _dsa_call(bq, iq, misc, dw, consts, bsz, t):
    nq = t // QBLK
    topk = min(TOPK_MAX, t // 4)
    blk = lambda b, j: (b * nq + j, 0)
    c2 = lambda b, j: (0, 0)
    return pl.pallas_call(
        functools.partial(_dsa_kernel, topk=topk),
        grid=(bsz, nq),
        in_specs=[pl.BlockSpec((QBLK, B_WIDTH), blk), pl.BlockSpec((QBLK, 512), blk),
                  pl.BlockSpec((t, 384), lambda b, j: (b, 0)),
                  pl.BlockSpec((1, KV_LATENT), c2),
                  pl.BlockSpec((B_WIDTH, B_HEADS * KV_LATENT), c2),
                  pl.BlockSpec((B_HEADS, HEAD_DIM, KV_LATENT), lambda b, j: (0, 0, 0)),
                  pl.BlockSpec((QBLK, QBLK), c2)],
        out_specs=pl.BlockSpec((QBLK, B_WIDTH), blk),
        out_shape=jax.ShapeDtypeStruct((bsz * t, B_WIDTH), BF16),
        scratch_shapes=[pltpu.VMEM((t, KV_LATENT), BF16), pltpu.VMEM((KV_LATENT, t), BF16),
                        pltpu.VMEM((t, QBLK), I32), pltpu.VMEM((t, B_HEADS * QBLK), F32),
                        pltpu.VMEM((IDX_HEADS // 2, 2 * QBLK, LANES), BF16),
                        pltpu.VMEM((B_HEADS * QBLK, KV_LATENT), BF16),
                        pltpu.VMEM((KV_LATENT, B_HEADS * QBLK), F32)],
        compiler_params=_cp(("parallel", "arbitrary")),
        name="dsa",
    )(bq, iq, misc, dw["kv_norm"], dw["wuk_bd"], dw["wuv_t"], consts["tri128"])


def _pad_cols(w, width):
    return jnp.pad(w, ((0, 0), (0, width - w.shape[1])))


def _pack_w_in(w_in, vres_w_in):
    a, b, c = w_in[:, :1280], w_in[:, 1280:2376], w_in[:, 2376:3400]
    mv = (jnp.zeros((D_MODEL, LANES), F32) if vres_w_in is None else _pad_cols(vres_w_in, LANES))
    ckv, ik, iw = b[:, 384:512], b[:, 1024:1088], b[:, 1088:1096]
    misc = jnp.concatenate([ckv, ik, ik, _pad_cols(iw, LANES)], axis=1)
    packed = jnp.concatenate([a[:, :1152], a[:, 1152:1280], mv, b[:, :384], b[:, 512:1024], misc, c],
                             axis=1)
    return packed.astype(BF16)


def _row(v):
    return v.reshape(1, -1).astype(F32)


def _block_diag(blocks):
    n = len(blocks)
    r, c = blocks[0].shape
    out = jnp.zeros((n * r, n * c), blocks[0].dtype)
    for i, blk in enumerate(blocks):
        out = out.at[i * r:(i + 1) * r, i * c:(i + 1) * c].set(blk)
    return out


def _constants(t):
    consts = {}
    i64 = jnp.arange(CHUNK)
    consts["tri"] = (i64[:, None] >= i64[None, :]).astype(BF16)
    i128 = jnp.arange(PAIR)
    consts["tri128"] = (i128[:, None] >= i128[None, :]).astype(BF16)
    consts["ones_pair"] = ((i128[:, None] // HEAD_DIM) == (i128[None, :] // HEAD_DIM)).astype(BF16)
    i384 = jnp.arange(A_WIDTH)
    consts["ones_a"] = ((i384[:, None] // HEAD_DIM) == (i384[None, :] // HEAD_DIM)).astype(BF16)
    r, c = i128[:, None], i128[None, :]
    lv = []
    for lvl in range(6):
        s = 1 << lvl
        lv.append(((r // (2 * s)) == (c // (2 * s))) & ((r % (2 * s)) >= s) & ((c % (2 * s)) < s))
    consts["levels"] = jnp.stack(lv).astype(F32)
    half = HEAD_DIM // 2
    theta = ROPE_BASE ** (-jnp.linspace(0.0, 1.0, half, dtype=F32))
    ang = jnp.arange(t, dtype=F32)[:, None] * theta[None, :]
    cos, sin = jnp.cos(ang), jnp.sin(ang)
    n_heads = C_WIDTH // HEAD_DIM
    consts["cos"] = jnp.tile(jnp.concatenate([cos, cos], axis=1), (1, n_heads))
    consts["sin"] = jnp.tile(jnp.concatenate([-sin, sin], axis=1), (1, n_heads))
    log_gamma = jnp.log(1.0 - 2.0 ** (-5.0 - jnp.arange(n_heads, dtype=F32)))
    n = jnp.arange(CHUNK, dtype=F32)
    diff = n[:, None] - n[None, :]
    intra = jnp.where(diff[None] >= 0,
                      jnp.exp(jnp.maximum(diff, 0.0)[None] * log_gamma[:, None, None]), 0.0)
    consts["ret_dmask"] = jnp.stack([_block_diag([intra[2 * p], intra[2 * p + 1]])
                                     for p in range(n_heads // 2)])
    lane_head = jnp.repeat(jnp.arange(n_heads), HEAD_DIM)
    consts["ret_qdec"] = jnp.exp((n[:, None] + 1.0) * log_gamma[None, :])[:, lane_head]
    consts["ret_kdec"] = jnp.exp((CHUNK - 1.0 - n)[:, None] * log_gamma[None, :])[:, lane_head]
    consts["ret_cdec"] = jnp.exp(CHUNK * log_gamma)[None, lane_head]
    return consts


def _pad_ff(wg, wu, wd):
    pad = D_FF_PAD - D_FF
    return (jnp.pad(wg, ((0, 0), (0, pad))).astype(BF16), jnp.pad(wu, ((0, 0), (0, pad))).astype(BF16),
            jnp.pad(wd, ((0, pad), (0, 0))).astype(BF16))


def kernel(x, ffn1_norm, ffn1_w_gate, ffn1_w_up, ffn1_w_down, mix_norm, w_in, w_out, rwkv_mu, rwkv_w0, rwkv_w2, rwkv_a0, rwkv_a2, rwkv_g2, rwkv_k_k, rwkv_k_a, rwkv_r_k, rwkv_ln_w, rwkv_ln_b, rwkv_vres_w_in, rwkv_vres_mu, rwkv_v0, rwkv_v2, dsa_kv_norm, dsa_w_uk, dsa_w_uv, ffn2_norm, ffn2_w_gate, ffn2_w_up, ffn2_w_down, final_norm):
    bsz, t, _ = x.shape
    depth = w_in.shape[0]
    consts = _constants(t)
    xf = x.reshape(bsz * t, D_MODEL)
    v_first = None
    for l in range(depth):
        xf = _ffn_call(xf, _row(ffn1_norm[l]), *_pad_ff(ffn1_w_gate[l], ffn1_w_up[l], ffn1_w_down[l]))
        w_packed = _pack_w_in(w_in[l], None if l == 0 else rwkv_vres_w_in[l - 1])
        rkv, lm, bq, iq, misc, pc = _mix_in_call(xf, _row(mix_norm[l]), w_packed)

        zeros_a = jnp.zeros((1, A_WIDTH), F32)
        w_lora = jnp.zeros((LANES, 3 * A_WIDTH), F32)
        w_lora = w_lora.at[0:32, 0:A_WIDTH].set(rwkv_w2[l])
        w_lora = w_lora.at[32:64, A_WIDTH:2 * A_WIDTH].set(rwkv_a2[l])
        w_lora = w_lora.at[64:128, 2 * A_WIDTH:].set(rwkv_g2[l])
        mu = rwkv_mu[l]
        mu_mv = jnp.zeros((LANES,), F32) if l == 0 else jnp.pad(rwkv_vres_mu[l - 1], (0, LANES - 16))
        pw = {
            "mu_rkv": _row(mu[:1152]),
            "mu_lm": _row(jnp.concatenate([mu[1152:1280], mu_mv])),
            "w0": _row(rwkv_w0[l]), "a0": _row(rwkv_a0[l]), "w_lora": w_lora.astype(BF16),
            "k_k": _row(rwkv_k_k[l]), "k_a": _row(rwkv_k_a[l]),
            "v0": zeros_a if l == 0 else _row(rwkv_v0[l - 1]),
            "v2": (jnp.zeros((LANES, A_WIDTH), BF16) if l == 0
                   else jnp.pad(rwkv_v2[l - 1], ((0, LANES - 16), (0, 0))).astype(BF16)),
            "ones_a": consts["ones_a"],
            "ln_w": _row(rwkv_ln_w[l]), "ln_b": _row(rwkv_ln_b[l]), "r_k": _row(rwkv_r_k[l]),
        }
        coef = _rwkv_prep_call(rkv, lm, v_first, pw, bsz, t)
        if l == 0:
            v_first = coef[2]
        o_a = _rwkv_core_call(coef, pw, consts, bsz, t)

        dw = {
            "kv_norm": _row(dsa_kv_norm[l]),
            "wuk_bd": _block_diag([dsa_w_uk[l, h] for h in range(B_HEADS)]).astype(BF16),
            "wuv_t": jnp.swapaxes(dsa_w_uv[l], 1, 2).astype(BF16),
        }
        o_b = _dsa_call(bq, iq, misc, dw, consts, bsz, t)

        o_c = _ret_call(pc, consts, bsz, t)

        wo = w_out[l].astype(BF16)
        mix = (o_a, o_b, o_c, wo[:A_WIDTH], wo[A_WIDTH:A_WIDTH + B_WIDTH], wo[A_WIDTH + B_WIDTH:])
        xf = _ffn_call(xf, _row(ffn2_norm[l]), *_pad_ff(ffn2_w_gate[l], ffn2_w_up[l], ffn2_w_down[l]),
                       mix=mix, final_g=_row(final_norm) if l == depth - 1 else None)
    return xf.reshape(bsz, t, D_MODEL)
```

```python
import functools
import math

import jax
import jax.numpy as jnp
from jax import lax
from jax.experimental import pallas as pl
from jax.experimental.pallas import tpu as pltpu

F32 = jnp.float32
BF16 = jnp.bfloat16
I32 = jnp.int32
I16 = jnp.int16

LANES = 128
HEAD_DIM = 64
PAIR = 2 * HEAD_DIM
CHUNK = 64
RWKV_CPI = 4
RET_CPI = 4
NORM_EPS = 1e-6
A_GN_EPS = 64e-5
D_MODEL = 1024
D_FF = 2752
D_FF_PAD = 2816
FF_TILE = 256
A_WIDTH = 384
B_WIDTH = 384
C_WIDTH = 256
KV_LATENT = 128
IDX_HEADS = 8
B_HEADS = 6
TOPK_MAX = 256
QBLK = 128
ROPE_BASE = 10000.0
NEG_BIG = -1e30
INT_MIN = -2147483648
VMEM_LIMIT = 52 * 1024 * 1024


def _cp(sem):
    return pltpu.CompilerParams(dimension_semantics=sem, vmem_limit_bytes=VMEM_LIMIT)


def _dot(a, b):
    return jnp.dot(a.astype(BF16), b.astype(BF16), preferred_element_type=F32)


def _dot_nt(a, b):
    return lax.dot_general(a.astype(BF16), b.astype(BF16), (((1,), (1,)), ((), ())),
                           preferred_element_type=F32)


def _dot_tn(a, b):
    return lax.dot_general(a.astype(BF16), b.astype(BF16), (((0,), (0,)), ((), ())),
                           preferred_element_type=F32)


def _dot_hl(m01, x):
    hi = x.astype(BF16)
    lo = (x - hi.astype(F32)).astype(BF16)
    n = x.shape[1]
    both = jnp.dot(m01, jnp.concatenate([hi, lo], axis=1), preferred_element_type=F32)
    return both[:, :n] + both[:, n:]


def _dot_hl_r(x, m01):
    hi = x.astype(BF16)
    lo = (x - hi.astype(F32)).astype(BF16)
    m = x.shape[0]
    both = jnp.dot(jnp.concatenate([hi, lo], axis=0), m01, preferred_element_type=F32)
    return both[:m] + both[m:]


def _rms(x, g):
    return x * lax.rsqrt(jnp.mean(x * x, axis=-1, keepdims=True) + NORM_EPS) * g


def _sigmoid(x):
    return 1.0 / (1.0 + jnp.exp(-x))


def _ffn_kernel(*refs, has_mix, final):
    it = iter(refs)
    x_ref = next(it)
    if has_mix:
        oa_ref, ob_ref, oc_ref, woa_ref, wob_ref, woc_ref = (next(it) for _ in range(6))
    g_ref, wg_ref, wu_ref, wd_ref = (next(it) for _ in range(4))
    fg_ref = next(it) if final else None
    out_ref = next(it)
    xs_ref = next(it)
    h_ref = next(it)
    j = pl.program_id(1)

    @pl.when(j == 0)
    def _():
        x = x_ref[...]
        if has_mix:
            x = (x + _dot(oa_ref[...], woa_ref[...]) + _dot(ob_ref[...], wob_ref[...])
                 + _dot(oc_ref[...], woc_ref[...]))
        xs_ref[...] = x
        h_ref[...] = _rms(x, g_ref[...]).astype(BF16)
        out_ref[...] = jnp.zeros_like(out_ref)

    h = h_ref[...]
    gate = jnp.dot(h, wg_ref[...], preferred_element_type=F32)
    up = jnp.dot(h, wu_ref[...], preferred_element_type=F32)
    act = (gate * _sigmoid(gate) * up).astype(BF16)
    out_ref[...] += jnp.dot(act, wd_ref[...], preferred_element_type=F32)

    @pl.when(j == pl.num_programs(1) - 1)
    def _():
        y = xs_ref[...] + 0.5 * out_ref[...]
        if final:
            y = _rms(y, fg_ref[...])
        out_ref[...] = y


def _ffn_call(x, norm_g, wg, wu, wd, mix=None, final_g=None):
    n = x.shape[0]
    tm = min(1024, n)
    nj = D_FF_PAD // FF_TILE
    row = lambda i, j: (i, 0)
    const = lambda i, j: (0, 0)
    in_specs = [pl.BlockSpec((tm, D_MODEL), row)]
    args = [x]
    if mix is not None:
        oa, ob, oc, woa, wob, woc = mix
        in_specs += [pl.BlockSpec((tm, A_WIDTH), row), pl.BlockSpec((tm, B_WIDTH), row),
                     pl.BlockSpec((tm, C_WIDTH), row),
                     pl.BlockSpec((A_WIDTH, D_MODEL), const), pl.BlockSpec((B_WIDTH, D_MODEL), const),
                     pl.BlockSpec((C_WIDTH, D_MODEL), const)]
        args += [oa, ob, oc, woa, wob, woc]
    in_specs += [pl.BlockSpec((1, D_MODEL), const),
                 pl.BlockSpec((D_MODEL, FF_TILE), lambda i, j: (0, j)),
                 pl.BlockSpec((D_MODEL, FF_TILE), lambda i, j: (0, j)),
                 pl.BlockSpec((FF_TILE, D_MODEL), lambda i, j: (j, 0))]
    args += [norm_g, wg, wu, wd]
    if final_g is not None:
        in_specs.append(pl.BlockSpec((1, D_MODEL), const))
        args.append(final_g)
    return pl.pallas_call(
        functools.partial(_ffn_kernel, has_mix=mix is not None, final=final_g is not None),
        grid=(n // tm, nj),
        in_specs=in_specs,
        out_specs=pl.BlockSpec((tm, D_MODEL), row),
        out_shape=jax.ShapeDtypeStruct((n, D_MODEL), F32),
        scratch_shapes=[pltpu.VMEM((tm, D_MODEL), F32), pltpu.VMEM((tm, D_MODEL), BF16)],
        compiler_params=_cp(("parallel", "arbitrary")),
        name="ffn",
    )(*args)


MIX_SLABS = (("rkv", 1152), ("lm", 256), ("bq", 384), ("iq", 512), ("misc", 384), ("pc", 1024))
MIX_COLS = sum(w for _, w in MIX_SLABS)


def _mix_in_kernel(x_ref, g_ref, w_ref, *out_refs):
    h = _rms(x_ref[...], g_ref[...]).astype(BF16)
    off = 0
    for ref, (_, width) in zip(out_refs, MIX_SLABS):
        step = 384 if width % 384 == 0 else 256
        for a in range(0, width, step):
            ref[:, a:a + step] = jnp.dot(h, w_ref[:, off + a:off + a + step],
                                         preferred_element_type=F32)
        off += width


def _mix_in_call(x, norm_g, w_packed):
    n = x.shape[0]
    tm = min(512, n)
    row = lambda i: (i, 0)
    const = lambda i: (0, 0)
    return pl.pallas_call(
        _mix_in_kernel,
        grid=(n // tm,),
        in_specs=[pl.BlockSpec((tm, D_MODEL), row), pl.BlockSpec((1, D_MODEL), const),
                  pl.BlockSpec((D_MODEL, MIX_COLS), const)],
        out_specs=[pl.BlockSpec((tm, w), row) for _, w in MIX_SLABS],
        out_shape=[jax.ShapeDtypeStruct((n, w), F32) for _, w in MIX_SLABS],
        compiler_params=_cp(("parallel",)),
        name="mix_in",
    )(x, norm_g, w_packed)


def _shift_lerp(p, mu, carry_ref, first_tile):
    rows = lax.broadcasted_iota(I32, p.shape, 0)
    carry = jnp.where(first_tile, jnp.zeros_like(carry_ref[0:1, :]), carry_ref[0:1, :])
    prev = jnp.where(rows == 0, carry, pltpu.roll(p, 1, axis=0))
    carry_ref[0:1, :] = p[p.shape[0] - 1:, :]
    return p + (prev - p) * mu


def _rwkv_prep_kernel(*refs, has_vres):
    it = iter(refs)
    rkv_ref, lm_ref = next(it), next(it)
    vfirst_ref = next(it) if has_vres else None
    (mu_rkv_ref, mu_lm_ref, w0_ref, a0_ref, wl_ref, kk_ref, ka_ref, v0_ref, v2_ref,
     ones_ref) = (next(it) for _ in range(10))
    r_ref, kp_ref, v_ref, nk_ref, beta_ref, lw_ref, g_ref = (next(it) for _ in range(7))
    c_rkv_ref, c_lm_ref = next(it), next(it)

    first = pl.program_id(1) == 0
    xa = _shift_lerp(rkv_ref[...], mu_rkv_ref[...], c_rkv_ref, first)
    xl = _shift_lerp(lm_ref[...], mu_lm_ref[...], c_lm_ref, first)
    r = xa[:, 0:A_WIDTH]
    k = xa[:, A_WIDTH:2 * A_WIDTH]
    v = xa[:, 2 * A_WIDTH:3 * A_WIDTH]
    lora = xl[:, 0:LANES]
    lane = lax.broadcasted_iota(I32, lora.shape, 1)
    feat = jnp.where(lane < 32, jnp.tanh(lora), jnp.where(lane < 64, lora, _sigmoid(lora)))
    proj = _dot(feat, wl_ref[...])
    w_pre = w0_ref[...] + proj[:, 0:A_WIDTH]
    sp = jnp.maximum(-w_pre, 0.0) + jnp.log(1.0 + jnp.exp(-jnp.abs(w_pre)))
    lw_ref[...] = -jnp.exp(-sp - 0.5)
    eta = _sigmoid(a0_ref[...] + proj[:, A_WIDTH:2 * A_WIDTH])
    g_ref[...] = proj[:, 2 * A_WIDTH:3 * A_WIDTH]
    kk = k * kk_ref[...]
    ss = _dot_hl_r(kk * kk, ones_ref[...])
    nk = kk / jnp.maximum(jnp.sqrt(ss), 1e-12)
    nk_ref[...] = nk
    beta_ref[...] = nk * eta
    kp_ref[...] = k * (1.0 + (eta - 1.0) * ka_ref[...])
    r_ref[...] = r
    if has_vres:
        mv = xl[:, LANES:2 * LANES]
        mix = _sigmoid(v0_ref[...] + _dot(mv, v2_ref[...]))
        v = v + (vfirst_ref[...] - v) * mix
    v_ref[...] = v


def _rwkv_prep_call(rkv, lm, vfirst, pw, bsz, t):
    tt = min(256, t)
    nt = t // tt
    tok = lambda b, i: (b * nt + i, 0)
    const = lambda b, i: (0, 0)
    has_vres = vfirst is not None
    in_specs = [pl.BlockSpec((tt, 1152), tok), pl.BlockSpec((tt, 256), tok)]
    args = [rkv, lm]
    if has_vres:
        in_specs.append(pl.BlockSpec((tt, A_WIDTH), tok))
        args.append(vfirst)
    params = [pw["mu_rkv"], pw["mu_lm"], pw["w0"], pw["a0"], pw["w_lora"], pw["k_k"], pw["k_a"],
              pw["v0"], pw["v2"], pw["ones_a"]]
    in_specs += [pl.BlockSpec(p.shape, const) for p in params]
    args += params
    n = bsz * t
    out = jax.ShapeDtypeStruct((n, A_WIDTH), F32)
    return pl.pallas_call(
        functools.partial(_rwkv_prep_kernel, has_vres=has_vres),
        grid=(bsz, nt),
        in_specs=in_specs,
        out_specs=[pl.BlockSpec((tt, A_WIDTH), tok)] * 7,
        out_shape=[out] * 7,
        scratch_shapes=[pltpu.VMEM((8, 1152), F32), pltpu.VMEM((8, 256), F32)],
        compiler_params=_cp(("parallel", "arbitrary")),
        name="rwkv_prep",
    )(*args)


def _bd(x, bd_mask):
    return jnp.where(bd_mask, jnp.concatenate([x, x], axis=0), 0.0)


def _chunk_masks():
    r = lax.broadcasted_iota(I32, (PAIR, PAIR), 0)
    c = lax.broadcasted_iota(I32, (PAIR, PAIR), 1)
    same = (r >> 6) == (c >> 6)
    return r, c, same


def _rwkv_core_kernel(r_ref, kp_ref, v_ref, nk_ref, beta_ref, lw_ref, g_ref,
                      lnw_ref, lnb_ref, rk_ref, tri_ref, lvl_ref, ones_ref,
                      o_ref, s_ref):
    @pl.when(pl.program_id(1) == 0)
    def _():
        s_ref[...] = jnp.zeros_like(s_ref)

    row, col, same = _chunk_masks()
    rl, cl = row & (CHUNK - 1), col & (CHUNK - 1)
    strict = same & (rl > cl)
    incl = same & (rl >= cl)
    eye = (row == col).astype(F32)
    tri = tri_ref[...]
    ones_pair = ones_ref[...]
    n_chunks = r_ref.shape[0] // CHUNK

    n_pairs = A_WIDTH // PAIR
    inst = [(ci, p) for ci in range(RWKV_CPI) for p in range(n_pairs)]
    each = lambda f, *cols: [f(*a) for a in zip(*cols)]

    def chunk_body(it, carry):
        rows = [pl.ds(pl.multiple_of((it * RWKV_CPI + ci) * CHUNK, CHUNK), CHUNK) for ci, _ in inst]
        sls = [slice(p * PAIR, (p + 1) * PAIR) for _, p in inst]
        ld = lambda ref: [ref[rw, sl] for rw, sl in zip(rows, sls)]
        r, kp, v, nk, beta, lw = (ld(ref) for ref in (r_ref, kp_ref, v_ref, nk_ref, beta_ref, lw_ref))
        cum = each(lambda x: _dot_hl(tri, x), lw)
        cum_l = each(lambda c: c[CHUNK - 1:CHUNK, :], cum)
        a_bd = each(lambda n, c, w: _bd(-n * jnp.exp(c - w), same), nk, cum, lw)
        r_bd = each(lambda x, c: _bd(x * jnp.exp(c), same), r, cum)
        g_inv = each(lambda c: jnp.exp(-c), cum)
        g_tail = each(lambda c, cl_: jnp.exp(cl_ - c), cum, cum_l)
        b_bd = each(lambda x, g: _bd(x * g, same), beta, g_inv)
        k_bd = each(lambda x, g: _bd(x * g, same), kp, g_inv)
        v_bd = each(lambda x: _bd(x, same), v)
        bg_bd = each(lambda x, g: _bd(x * g, same), beta, g_tail)
        kg_bd = each(lambda x, g: _bd(x * g, same), kp, g_tail)

        ar = each(lambda a, b: jnp.concatenate([a, b], axis=0), a_bd, r_bd)
        g1 = each(_dot_nt, ar, b_bd)
        g2 = each(_dot_nt, ar, k_bd)
        a_ab = each(lambda g: jnp.where(strict, g[:PAIR], 0.0), g1)
        a_rb = each(lambda g: jnp.where(incl, g[PAIR:], 0.0), g1)
        a_ak = each(lambda g: jnp.where(strict, g[:PAIR], 0.0), g2)
        a_rk = each(lambda g: jnp.where(incl, g[PAIR:], 0.0), g2)

        t_inv = each(lambda a: eye + a * lvl_ref[0], a_ab)
        w_loc = each(_dot, a_ak, v_bd)
        rkv = each(_dot, a_rk, v_bd)
        n_loc = each(_dot_tn, v_bd, kg_bd)
        for lvl in range(1, 6):
            x = each(lambda a, t_: _dot(a * lvl_ref[lvl], t_), a_ab, t_inv)
            t_inv = each(lambda t_, x_: t_ + _dot(t_, x_), t_inv, x)

        au = each(lambda t_, a, w: _dot(t_, jnp.concatenate([a, w], axis=1)), t_inv, a_bd, w_loc)
        ry = each(lambda rb, y0, arb, au_: jnp.concatenate([rb, y0], axis=1) + _dot(arb, au_),
                  r_bd, rkv, a_rb, au)
        mn = each(_dot_tn, au, bg_bd)

        y = [None] * len(inst)
        for ci in range(RWKV_CPI):
            idx = [i for i, (cj, _) in enumerate(inst) if cj == ci]
            s = [s_ref[inst[i][1]] for i in idx]
            y_bd = [_dot_nt(ry[i][:, :PAIR], s_) + ry[i][:, PAIR:] for i, s_ in zip(idx, s)]
            s_new = [s_ * jnp.exp(cum_l[i]) + _dot(s_, mn[i][:PAIR]) + mn[i][PAIR:] + n_loc[i]
                     for i, s_ in zip(idx, s)]
            for i, sn, yb in zip(idx, s_new, y_bd):
                s_ref[inst[i][1]] = sn
                y[i] = yb[:CHUNK] + yb[CHUNK:]

        mean = each(lambda y_: _dot_hl_r(y_, ones_pair) * (1.0 / HEAD_DIM), y)
        bonus = each(lambda r_, k_, sl: _dot_hl_r(r_ * k_ * rk_ref[:, sl], ones_pair), r, kp, sls)
        yc = each(lambda y_, m_: y_ - m_, y, mean)
        var = each(lambda c: _dot_hl_r(c * c, ones_pair) * (1.0 / HEAD_DIM), yc)
        for i, (rw, sl) in enumerate(zip(rows, sls)):
            yn = yc[i] * lax.rsqrt(var[i] + A_GN_EPS) * lnw_ref[:, sl] + lnb_ref[:, sl]
            o_ref[rw, sl] = ((yn + bonus[i] * v[i]) * g_ref[rw, sl]).astype(o_ref.dtype)
        return carry

    lax.fori_loop(0, n_chunks // RWKV_CPI, chunk_body, 0)


def _rwkv_core_call(coef, pw, consts, bsz, t):
    tt = min(512, t)
    nt = t // tt
    tok = lambda b, i: (b * nt + i, 0)
    c2 = lambda b, i: (0, 0)
    c3 = lambda b, i: (0, 0, 0)
    params = [pw["ln_w"], pw["ln_b"], pw["r_k"]]
    return pl.pallas_call(
        _rwkv_core_kernel,
        grid=(bsz, nt),
        in_specs=([pl.BlockSpec((tt, A_WIDTH), tok)] * 7
                  + [pl.BlockSpec((1, A_WIDTH), c2)] * 3
                  + [pl.BlockSpec((CHUNK, CHUNK), c2), pl.BlockSpec((6, PAIR, PAIR), c3),
                     pl.BlockSpec((PAIR, PAIR), c2)]),
        out_specs=pl.BlockSpec((tt, A_WIDTH), tok),
        out_shape=jax.ShapeDtypeStruct((bsz * t, A_WIDTH), BF16),
        scratch_shapes=[pltpu.VMEM((A_WIDTH // PAIR, PAIR, PAIR), F32)],
        compiler_params=_cp(("parallel", "arbitrary")),
        name="rwkv_core",
    )(*coef, *params, consts["tri"], consts["levels"], consts["ones_pair"])


def _swap_halves(z):
    lane = lax.broadcasted_iota(I32, z.shape, 1)
    return jnp.where((lane & (HEAD_DIM // 2)) == 0,
                     pltpu.roll(z, LANES - HEAD_DIM // 2, axis=1),
                     pltpu.roll(z, HEAD_DIM // 2, axis=1))


def _ret_kernel(pc_ref, cos_ref, sin_ref, dmask_ref, qdec_ref, kdec_ref, cdec_ref, ones_ref,
                o_ref, z_ref):
    @pl.when(pl.program_id(1) == 0)
    def _():
        z_ref[...] = jnp.zeros_like(z_ref)

    _, _, same = _chunk_masks()
    ones_pair = ones_ref[...]
    n_chunks = pc_ref.shape[0] // CHUNK

    n_pairs = C_WIDTH // PAIR
    inst = [(ci, p) for ci in range(RET_CPI) for p in range(n_pairs)]
    each = lambda f, *cols: [f(*a) for a in zip(*cols)]

    def chunk_body(it, carry):
        rows = [pl.ds(pl.multiple_of((it * RET_CPI + ci) * CHUNK, CHUNK), CHUNK) for ci, _ in inst]
        sls = [slice(p * PAIR, (p + 1) * PAIR) for _, p in inst]
        cos = [cos_ref[rw, sl] for rw, sl in zip(rows, sls)]
        sin = [sin_ref[rw, sl] for rw, sl in zip(rows, sls)]
        part = lambda j: [pc_ref[rw, j * C_WIDTH + p * PAIR:j * C_WIDTH + (p + 1) * PAIR]
                          for rw, (_, p) in zip(rows, inst)]
        q, k, v = part(0), part(1), part(2)
        qr = each(lambda z, c, s: z * c + _swap_halves(z) * s, q, cos, sin)
        kr = each(lambda z, c, s: (z * c + _swap_halves(z) * s) * (HEAD_DIM ** -0.5), k, cos, sin)
        q_bd = each(lambda z: _bd(z, same), qr)
        k_bd = each(lambda z: _bd(z, same), kr)
        v_bd = each(lambda z: _bd(z, same), v)
        qd_bd = each(lambda z, sl: _bd(z * qdec_ref[:, sl], same), qr, sls)
        kd_bd = each(lambda z, sl: _bd(z * kdec_ref[:, sl], same), kr, sls)
        scores = each(lambda a, b, ip: _dot_nt(a, b) * dmask_ref[ip[1]], q_bd, k_bd, inst)
        inner = each(_dot, scores, v_bd)
        kv = each(_dot_tn, kd_bd, v_bd)
        y = [None] * len(inst)
        for ci in range(RET_CPI):
            idx = [i for i, (cj, _) in enumerate(inst) if cj == ci]
            z = [z_ref[inst[i][1]] for i in idx]
            cross = [_dot(qd_bd[i], z_) for i, z_ in zip(idx, z)]
            for i, z_, cr in zip(idx, z, cross):
                z_ref[inst[i][1]] = z_ * cdec_ref[:, sls[i]] + kv[i]
                yb = inner[i] + cr
                y[i] = yb[:CHUNK] + yb[CHUNK:]
        ms = each(lambda y_: _dot_hl_r(y_ * y_, ones_pair) * (1.0 / HEAD_DIM), y)
        g = part(3)
        for i, (rw, sl) in enumerate(zip(rows, sls)):
            o_ref[rw, sl] = (g[i] * _sigmoid(g[i]) * y[i] * lax.rsqrt(ms[i] + NORM_EPS)).astype(o_ref.dtype)
        return carry

    lax.fori_loop(0, n_chunks // RET_CPI, chunk_body, 0)


def _ret_call(pc, consts, bsz, t):
    tt = min(512, t)
    nt = t // tt
    tok = lambda b, i: (b * nt + i, 0)
    c2 = lambda b, i: (0, 0)
    return pl.pallas_call(
        _ret_kernel,
        grid=(bsz, nt),
        in_specs=[pl.BlockSpec((tt, 4 * C_WIDTH), tok),
                  pl.BlockSpec((tt, C_WIDTH), lambda b, i: (i, 0)),
                  pl.BlockSpec((tt, C_WIDTH), lambda b, i: (i, 0)),
                  pl.BlockSpec((C_WIDTH // PAIR, PAIR, PAIR), lambda b, i: (0, 0, 0)),
                  pl.BlockSpec((CHUNK, C_WIDTH), c2), pl.BlockSpec((CHUNK, C_WIDTH), c2),
                  pl.BlockSpec((1, C_WIDTH), c2), pl.BlockSpec((PAIR, PAIR), c2)],
        out_specs=pl.BlockSpec((tt, C_WIDTH), tok),
        out_shape=jax.ShapeDtypeStruct((bsz * t, C_WIDTH), BF16),
        scratch_shapes=[pltpu.VMEM((C_WIDTH // PAIR, PAIR, PAIR), F32)],
        compiler_params=_cp(("parallel", "arbitrary")),
        name="retention",
    )(pc, consts["cos"], consts["sin"], consts["ret_dmask"], consts["ret_qdec"],
      consts["ret_kdec"], consts["ret_cdec"], consts["ones_pair"])


def _dsa_kernel(bq_ref, iq_ref, misc_ref, kvn_ref, wuk_ref, wuvt_ref, tri_ref,
                o_ref, c_ref, ct_ref, key_ref, khi_ref, klo_ref, logit_ref, rhs_ref, qls_ref, acc_ref,
                *, topk):
    j = pl.program_id(1)
    t = misc_ref.shape[0]
    nkb = j + 1
    n_heads = B_HEADS

    @pl.when(j == 0)
    def _():
        def body(i, carry):
            rows = pl.ds(pl.multiple_of(i * QBLK, QBLK), QBLK)
            c = _rms(misc_ref[rows, 0:KV_LATENT], kvn_ref[...])
            c_ref[rows, :] = c.astype(BF16)
            ct_ref[:, rows] = c.T.astype(BF16)
            return carry
        lax.fori_loop(0, t // QBLK, body, 0)

    qrows = pl.ds(pl.multiple_of(j * QBLK, QBLK), QBLK)
    qlat = _dot(bq_ref[...], wuk_ref[...]) * (HEAD_DIM ** -0.5)
    for h in range(n_heads):
        qls_ref[h * QBLK:(h + 1) * QBLK, :] = qlat[:, h * LANES:(h + 1) * LANES].astype(BF16)
    iq = iq_ref[...] * (HEAD_DIM ** -0.5)
    lane = lax.broadcasted_iota(I32, (QBLK, LANES), 1)
    for hp in range(IDX_HEADS // 2):
        slab = iq[:, hp * LANES:(hp + 1) * LANES]
        rhs_ref[hp, 0:QBLK, :] = jnp.where(lane < HEAD_DIM, slab, 0.0).astype(BF16)
        rhs_ref[hp, QBLK:2 * QBLK, :] = jnp.where(lane >= HEAD_DIM, slab, 0.0).astype(BF16)
    iw_t = misc_ref[qrows, 2 * LANES:3 * LANES].T * (IDX_HEADS ** -0.5)

    srow = lax.broadcasted_iota(I32, (QBLK, QBLK), 0)
    qcol = lax.broadcasted_iota(I32, (QBLK, QBLK), 1)
    qpos = j * QBLK + qcol

    def visible(kb):
        return (kb * QBLK + srow) <= qpos

    n_units = (nkb + 1) // 2
    halves = lambda u: [2 * u, 2 * u + 1]
    blk_rows = lambda kb: pl.ds(pl.multiple_of(kb * QBLK, QBLK), QBLK)
    unit_rows = lambda u: pl.ds(pl.multiple_of(u * 2 * QBLK, 2 * QBLK), 2 * QBLK)

    def score_body(u, carry):
        kbs = halves(u)
        ik2 = [misc_ref[blk_rows(kb), LANES:2 * LANES].astype(BF16) for kb in kbs]
        acc = [jnp.zeros((QBLK, QBLK), F32) for _ in kbs]
        for hp in range(IDX_HEADS // 2):
            rel = [lax.dot_general(x, rhs_ref[hp], (((1,), (1,)), ((), ())),
                                   preferred_element_type=F32) for x in ik2]
            acc = [a + iw_t[2 * hp:2 * hp + 1, :] * jnp.maximum(r_[:, :QBLK], 0.0)
                   + iw_t[2 * hp + 1:2 * hp + 2, :] * jnp.maximum(r_[:, QBLK:], 0.0)
                   for a, r_ in zip(acc, rel)]
        for kb, a in zip(kbs, acc):
            score = jnp.where(visible(kb), a + 0.0, -jnp.inf)
            bits = pltpu.bitcast(score, I32)
            key = jnp.where(bits < 0, bits ^ 0x7FFFFFFF, bits)
            key_ref[blk_rows(kb), :] = key
            khi_ref[blk_rows(kb), :] = (key >> 16).astype(I16)
        return carry
    lax.fori_loop(0, n_units, score_body, 0)

    def count16(ref, pred):
        def body(u, acc):
            return acc + jnp.where(pred(ref[unit_rows(u), :]), jnp.int16(1), jnp.int16(0))
        acc = lax.fori_loop(0, n_units, body, jnp.zeros((2 * QBLK, QBLK), I16))
        return jnp.sum(acc.astype(I32), axis=0, keepdims=True)

    def bisect16(ref, kth, start):
        c0 = count16(ref, lambda k: k >= jnp.int16(0))
        v0 = jnp.where(c0 >= kth, 0, start).astype(I32)

        def bit_body(i, v):
            cand = v | jnp.left_shift(jnp.int32(1), 14 - i)
            cnt = count16(ref, lambda k: k >= cand.astype(I16))
            return jnp.where(cnt >= kth, cand, v)
        return lax.fori_loop(0, 15, bit_body, v0)

    tau_hi = bisect16(khi_ref, topk, -32768)
    n_gt_hi = count16(khi_ref, lambda k: k > tau_hi.astype(I16))

    def lo_body(u, carry):
        rows = unit_rows(u)
        key = key_ref[rows, :]
        lo = ((key & 0xFFFF) - 32768).astype(I16)
        klo_ref[rows, :] = jnp.where((key >> 16) == tau_hi, lo, jnp.int16(-32768))
        return carry
    lax.fori_loop(0, n_units, lo_body, 0)
    tau_lo = bisect16(klo_ref, topk - n_gt_hi, -32768)
    tau = tau_hi * 65536 + (tau_lo + 32768)
    n_gt = n_gt_hi + count16(klo_ref, lambda k: k > tau_lo.astype(I16))
    need = (topk - n_gt).astype(F32)

    tri = tri_ref[...]

    def logit_body(u, carry):
        seen, m = carry
        kbs = halves(u)
        keys = [key_ref[blk_rows(kb), :] for kb in kbs]
        eqs = [k == tau for k in keys]
        pref = [jnp.dot(tri, jnp.where(e, 1.0, 0.0).astype(BF16), preferred_element_type=F32)
                for e in eqs]
        logits = lax.dot_general(c_ref[unit_rows(u), :], qls_ref[...], (((1,), (1,)), ((), ())),
                                 preferred_element_type=F32)
        for i, kb in enumerate(kbs):
            rank = seen + pref[i]
            seen = rank[QBLK - 1:QBLK, :]
            sel = visible(kb) & ((keys[i] > tau) | (eqs[i] & (rank <= need)))
            new_m = []
            for h in range(n_heads):
                lh = jnp.where(sel, logits[i * QBLK:(i + 1) * QBLK, h * QBLK:(h + 1) * QBLK], NEG_BIG)
                logit_ref[blk_rows(kb), h * QBLK:(h + 1) * QBLK] = lh
                new_m.append(jnp.max(lh, axis=0, keepdims=True))
            m = jnp.maximum(m, jnp.concatenate(new_m, axis=1))
        return seen, m

    _, m = lax.fori_loop(0, n_units, logit_body,
                         (jnp.zeros((1, QBLK), F32), jnp.full((1, n_heads * QBLK), NEG_BIG, F32)))

    acc_ref[...] = jnp.zeros_like(acc_ref)

    def pv_body(u, lsum):
        rows = unit_rows(u)
        p = jnp.exp(logit_ref[rows, :] - m)
        acc_ref[...] += jnp.dot(ct_ref[:, rows], p.astype(BF16), preferred_element_type=F32)
        return lsum + jnp.sum(p, axis=0, keepdims=True)
    lsum = lax.fori_loop(0, n_units, pv_body, jnp.zeros((1, n_heads * QBLK), F32))

    o_lat = acc_ref[...] / lsum
    outs = [jnp.dot(wuvt_ref[h], o_lat[:, h * QBLK:(h + 1) * QBLK].astype(BF16),
                    preferred_element_type=F32) for h in range(n_heads)]
    o_ref[...] = jnp.concatenate(outs, axis=0).T.astype(o_ref.dtype)


def _dsa_call(bq, iq, misc, dw, consts, bsz, t):
    nq = t // QBLK
    topk = min(TOPK_MAX, t // 4)
    blk = lambda b, j: (b * nq + j, 0)
    c2 = lambda b, j: (0, 0)
    return pl.pallas_call(
        functools.partial(_dsa_kernel, topk=topk),
        grid=(bsz, nq),
        in_specs=[pl.BlockSpec((QBLK, B_WIDTH), blk), pl.BlockSpec((QBLK, 512), blk),
                  pl.BlockSpec((t, 384), lambda b, j: (b, 0)),
                  pl.BlockSpec((1, KV_LATENT), c2),
                  pl.BlockSpec((B_WIDTH, B_HEADS * KV_LATENT), c2),
                  pl.BlockSpec((B_HEADS, HEAD_DIM, KV_LATENT), lambda b, j: (0, 0, 0)),
                  pl.BlockSpec((QBLK, QBLK), c2)],
        out_specs=pl.BlockSpec((QBLK, B_WIDTH), blk),
        out_shape=jax.ShapeDtypeStruct((bsz * t, B_WIDTH), BF16),
        scratch_shapes=[pltpu.VMEM((t, KV_LATENT), BF16), pltpu.VMEM((KV_LATENT, t), BF16),
                        pltpu.VMEM((t, QBLK), I32), pltpu.VMEM((t, QBLK), I16), pltpu.VMEM((t, QBLK), I16),
                        pltpu.VMEM((t, B_HEADS * QBLK), F32),
                        pltpu.VMEM((IDX_HEADS // 2, 2 * QBLK, LANES), BF16),
                        pltpu.VMEM((B_HEADS * QBLK, KV_LATENT), BF16),
                        pltpu.VMEM((KV_LATENT, B_HEADS * QBLK), F32)],
        compiler_params=_cp(("parallel", "arbitrary")),
        name="dsa",
    )(bq, iq, misc, dw["kv_norm"], dw["wuk_bd"], dw["wuv_t"], consts["tri128"])


def _pad_cols(w, width):
    return jnp.pad(w, ((0, 0), (0, width - w.shape[1])))


def _pack_w_in(w_in, vres_w_in):
    a, b, c = w_in[:, :1280], w_in[:, 1280:2376], w_in[:, 2376:3400]
    mv = (jnp.zeros((D_MODEL, LANES), F32) if vres_w_in is None else _pad_cols(vres_w_in, LANES))
    ckv, ik, iw = b[:, 384:512], b[:, 1024:1088], b[:, 1088:1096]
    misc = jnp.concatenate([ckv, ik, ik, _pad_cols(iw, LANES)], axis=1)
    packed = jnp.concatenate([a[:, :1152], a[:, 1152:1280], mv, b[:, :384], b[:, 512:1024], misc, c],
                             axis=1)
    return packed.astype(BF16)


def _row(v):
    return v.reshape(1, -1).astype(F32)


def _block_diag(blocks):
    n = len(blocks)
    r, c = blocks[0].shape
    out = jnp.zeros((n * r, n * c), blocks[0].dtype)
    for i, blk in enumerate(blocks):
        out = out.at[i * r:(i + 1) * r, i * c:(i + 1) * c].set(blk)
    return out


def _constants(t):
    consts = {}
    i64 = jnp.arange(CHUNK)
    consts["tri"] = (i64[:, None] >= i64[None, :]).astype(BF16)
    i128 = jnp.arange(PAIR)
    consts["tri128"] = (i128[:, None] >= i128[None, :]).astype(BF16)
    consts["ones_pair"] = ((i128[:, None] // HEAD_DIM) == (i128[None, :] // HEAD_DIM)).astype(BF16)
    i384 = jnp.arange(A_WIDTH)
    consts["ones_a"] = ((i384[:, None] // HEAD_DIM) == (i384[None, :] // HEAD_DIM)).astype(BF16)
    r, c = i128[:, None], i128[None, :]
    lv = []
    for lvl in range(6):
        s = 1 << lvl
        lv.append(((r // (2 * s)) == (c // (2 * s))) & ((r % (2 * s)) >= s) & ((c % (2 * s)) < s))
    consts["levels"] = jnp.stack(lv).astype(F32)
    half = HEAD_DIM // 2
    theta = ROPE_BASE ** (-jnp.linspace(0.0, 1.0, half, dtype=F32))
    ang = jnp.arange(t, dtype=F32)[:, None] * theta[None, :]
    cos, sin = jnp.cos(ang), jnp.sin(ang)
    n_heads = C_WIDTH // HEAD_DIM
    consts["cos"] = jnp.tile(jnp.concatenate([cos, cos], axis=1), (1, n_heads))
    consts["sin"] = jnp.tile(jnp.concatenate([-sin, sin], axis=1), (1, n_heads))
    log_gamma = jnp.log(1.0 - 2.0 ** (-5.0 - jnp.arange(n_heads, dtype=F32)))
    n = jnp.arange(CHUNK, dtype=F32)
    diff = n[:, None] - n[None, :]
    intra = jnp.where(diff[None] >= 0,
                      jnp.exp(jnp.maximum(diff, 0.0)[None] * log_gamma[:, None, None]), 0.0)
    consts["ret_dmask"] = jnp.stack([_block_diag([intra[2 * p], intra[2 * p + 1]])
                                     for p in range(n_heads // 2)])
    lane_head = jnp.repeat(jnp.arange(n_heads), HEAD_DIM)
    consts["ret_qdec"] = jnp.exp((n[:, None] + 1.0) * log_gamma[None, :])[:, lane_head]
    consts["ret_kdec"] = jnp.exp((CHUNK - 1.0 - n)[:, None] * log_gamma[None, :])[:, lane_head]
    consts["ret_cdec"] = jnp.exp(CHUNK * log_gamma)[None, lane_head]
    return consts


def _pad_ff(wg, wu, wd):
    pad = D_FF_PAD - D_FF
    return (jnp.pad(wg, ((0, 0), (0, pad))).astype(BF16), jnp.pad(wu, ((0, 0), (0, pad))).astype(BF16),
            jnp.pad(wd, ((0, pad), (0, 0))).astype(BF16))


def kernel(x, ffn1_norm, ffn1_w_gate, ffn1_w_up, ffn1_w_down, mix_norm, w_in, w_out, rwkv_mu, rwkv_w0, rwkv_w2, rwkv_a0, rwkv_a2, rwkv_g2, rwkv_k_k, rwkv_k_a, rwkv_r_k, rwkv_ln_w, rwkv_ln_b, rwkv_vres_w_in, rwkv_vres_mu, rwkv_v0, rwkv_v2, dsa_kv_norm, dsa_w_uk, dsa_w_uv, ffn2_norm, ffn2_w_gate, ffn2_w_up, ffn2_w_down, final_norm):
    bsz, t, _ = x.shape
    depth = w_in.shape[0]
    consts = _constants(t)
    xf = x.reshape(bsz * t, D_MODEL)
    v_first = None
    for l in range(depth):
        xf = _ffn_call(xf, _row(ffn1_norm[l]), *_pad_ff(ffn1_w_gate[l], ffn1_w_up[l], ffn1_w_down[l]))
        w_packed = _pack_w_in(w_in[l], None if l == 0 else rwkv_vres_w_in[l - 1])
        rkv, lm, bq, iq, misc, pc = _mix_in_call(xf, _row(mix_norm[l]), w_packed)

        zeros_a = jnp.zeros((1, A_WIDTH), F32)
        w_lora = jnp.zeros((LANES, 3 * A_WIDTH), F32)
        w_lora = w_lora.at[0:32, 0:A_WIDTH].set(rwkv_w2[l])
        w_lora = w_lora.at[32:64, A_WIDTH:2 * A_WIDTH].set(rwkv_a2[l])
        w_lora = w_lora.at[64:128, 2 * A_WIDTH:].set(rwkv_g2[l])
        mu = rwkv_mu[l]
        mu_mv = jnp.zeros((LANES,), F32) if l == 0 else jnp.pad(rwkv_vres_mu[l - 1], (0, LANES - 16))
        pw = {
            "mu_rkv": _row(mu[:1152]),
            "mu_lm": _row(jnp.concatenate([mu[1152:1280], mu_mv])),
            "w0": _row(rwkv_w0[l]), "a0": _row(rwkv_a0[l]), "w_lora": w_lora.astype(BF16),
            "k_k": _row(rwkv_k_k[l]), "k_a": _row(rwkv_k_a[l]),
            "v0": zeros_a if l == 0 else _row(rwkv_v0[l - 1]),
            "v2": (jnp.zeros((LANES, A_WIDTH), BF16) if l == 0
                   else jnp.pad(rwkv_v2[l - 1], ((0, LANES - 16), (0, 0))).astype(BF16)),
            "ones_a": consts["ones_a"],
            "ln_w": _row(rwkv_ln_w[l]), "ln_b": _row(rwkv_ln_b[l]), "r_k": _row(rwkv_r_k[l]),
        }
        coef = _rwkv_prep_call(rkv, lm, v_first, pw, bsz, t)
        if l == 0:
            v_first = coef[2]
        o_a = _rwkv_core_call(coef, pw, consts, bsz, t)

        dw = {
            "kv_norm": _row(dsa_kv_norm[l]),
            "wuk_bd": _block_diag([dsa_w_uk[l, h] for h in range(B_HEADS)]).astype(BF16),
            "wuv_t": jnp.swapaxes(dsa_w_uv[l], 1, 2).astype(BF16),
        }
        o_b = _dsa_call(bq, iq, misc, dw, consts, bsz, t)

        o_c = _ret_call(pc, consts, bsz, t)

        wo = w_out[l].astype(BF16)
        mix = (o_a, o_b, o_c, wo[:A_WIDTH], wo[A_WIDTH:A_WIDTH + B_WIDTH], wo[A_WIDTH + B_WIDTH:])
        xf = _ffn_call(xf, _row(ffn2_norm[l]), *_pad_ff(ffn2_w_gate[l], ffn2_w_up[l], ffn2_w_down[l]),
                       mix=mix, final_g=_row(final_norm) if l == depth - 1 else None)
    return xf.reshape(bsz, t, D_MODEL)
```

```python
import functools
import math

import jax
import jax.numpy as jnp
from jax import lax
from jax.experimental import pallas as pl
from jax.experimental.pallas import tpu as pltpu

F32 = jnp.float32
BF16 = jnp.bfloat16
I32 = jnp.int32

LANES = 128
MXU_N = 256
HEAD_DIM = 64
PAIR = 2 * HEAD_DIM
CHUNK = 64
RWKV_CPI = 4
RET_CPI = 4
NORM_EPS = 1e-6
A_GN_EPS = 64e-5
D_MODEL = 1024
D_FF = 2752
D_FF_PAD = 2816
FF_TILE = 256
A_WIDTH = 384
B_WIDTH = 384
C_WIDTH = 256
KV_LATENT = 128
IDX_HEADS = 8
B_HEADS = 6
TOPK_MAX = 256
QBLK = 128
QB = 256
ROPE_BASE = 10000.0
NEG_BIG = -1e30
INT_MIN = -2147483648
VMEM_LIMIT = 52 * 1024 * 1024


def _cp(sem):
    return pltpu.CompilerParams(dimension_semantics=sem, vmem_limit_bytes=VMEM_LIMIT)


def _dot(a, b):
    return jnp.dot(a.astype(BF16), b.astype(BF16), preferred_element_type=F32)


def _dot_nt(a, b):
    return lax.dot_general(a.astype(BF16), b.astype(BF16), (((1,), (1,)), ((), ())),
                           preferred_element_type=F32)


def _dot_tn(a, b):
    return lax.dot_general(a.astype(BF16), b.astype(BF16), (((0,), (0,)), ((), ())),
                           preferred_element_type=F32)


def _dot_hl(m01, x):
    hi = x.astype(BF16)
    lo = (x - hi.astype(F32)).astype(BF16)
    n = x.shape[1]
    both = jnp.dot(m01, jnp.concatenate([hi, lo], axis=1), preferred_element_type=F32)
    return both[:, :n] + both[:, n:]


def _dot_hl_r(x, m01):
    hi = x.astype(BF16)
    lo = (x - hi.astype(F32)).astype(BF16)
    m = x.shape[0]
    both = jnp.dot(jnp.concatenate([hi, lo], axis=0), m01, preferred_element_type=F32)
    return both[:m] + both[m:]


def _rms(x, g):
    return x * lax.rsqrt(jnp.mean(x * x, axis=-1, keepdims=True) + NORM_EPS) * g


def _sigmoid(x):
    return 1.0 / (1.0 + jnp.exp(-x))


def _ffn_kernel(*refs, has_mix, final):
    it = iter(refs)
    x_ref = next(it)
    if has_mix:
        oa_ref, ob_ref, oc_ref, woa_ref, wob_ref, woc_ref = (next(it) for _ in range(6))
    g_ref, wg_ref, wu_ref, wd_ref = (next(it) for _ in range(4))
    fg_ref = next(it) if final else None
    out_ref = next(it)
    h_ref = next(it)
    xs_ref = next(it) if has_mix else x_ref
    j = pl.program_id(1)

    @pl.when(j == 0)
    def _():
        x = x_ref[...]
        if has_mix:
            x = (x + _dot(oa_ref[...], woa_ref[...]) + _dot(ob_ref[...], wob_ref[...])
                 + _dot(oc_ref[...], woc_ref[...]))
            xs_ref[...] = x
        h_ref[...] = _rms(x, g_ref[...]).astype(BF16)

    h = h_ref[...]
    gate = jnp.dot(h, wg_ref[...], preferred_element_type=F32)
    up = jnp.dot(h, wu_ref[...], preferred_element_type=F32)
    act = (gate * _sigmoid(gate) * up).astype(BF16)

    @pl.when(j == 0)
    def _():
        out_ref[...] = jnp.dot(act, wd_ref[...], preferred_element_type=F32)

    @pl.when(j > 0)
    def _():
        out_ref[...] += jnp.dot(act, wd_ref[...], preferred_element_type=F32)

    @pl.when(j == pl.num_programs(1) - 1)
    def _():
        y = xs_ref[...] + 0.5 * out_ref[...]
        if final:
            y = _rms(y, fg_ref[...])
        out_ref[...] = y


def _ffn_call(x, norm_g, weights, layer, mix=None, final_g=None):
    wg, wu, wd = weights
    n = x.shape[0]
    tm = min(1024, n)
    nj = D_FF_PAD // FF_TILE
    row = lambda i, j: (i, 0)
    const = lambda i, j: (0, 0)
    in_specs = [pl.BlockSpec((tm, D_MODEL), row)]
    args = [x]
    if mix is not None:
        oa, ob, oc, woa, wob, woc = mix
        in_specs += [pl.BlockSpec((tm, A_WIDTH), row), pl.BlockSpec((tm, B_WIDTH), row),
                     pl.BlockSpec((tm, C_WIDTH), row),
                     pl.BlockSpec((A_WIDTH, D_MODEL), const), pl.BlockSpec((B_WIDTH, D_MODEL), const),
                     pl.BlockSpec((C_WIDTH, D_MODEL), const)]
        args += [oa, ob, oc, woa, wob, woc]
    in_specs += [pl.BlockSpec((1, D_MODEL), const),
                 pl.BlockSpec((None, D_MODEL, FF_TILE), lambda i, j: (layer, 0, j)),
                 pl.BlockSpec((None, D_MODEL, FF_TILE), lambda i, j: (layer, 0, j)),
                 pl.BlockSpec((None, FF_TILE, D_MODEL), lambda i, j: (layer, j, 0))]
    args += [norm_g, wg, wu, wd]
    if final_g is not None:
        in_specs.append(pl.BlockSpec((1, D_MODEL), const))
        args.append(final_g)
    scratch = [pltpu.VMEM((tm, D_MODEL), BF16)]
    if mix is not None:
        scratch.append(pltpu.VMEM((tm, D_MODEL), F32))
    return pl.pallas_call(
        functools.partial(_ffn_kernel, has_mix=mix is not None, final=final_g is not None),
        grid=(n // tm, nj),
        in_specs=in_specs,
        out_specs=pl.BlockSpec((tm, D_MODEL), row),
        out_shape=jax.ShapeDtypeStruct((n, D_MODEL), F32),
        scratch_shapes=scratch,
        compiler_params=_cp(("parallel", "arbitrary")),
        name="ffn",
    )(*args)


MIX_SLABS = (("rkv", 1152), ("lm", 256), ("bq", 384), ("iq", 512), ("misc", 384), ("pc", 1024))
MIX_COLS = sum(w for _, w in MIX_SLABS)


def _mix_in_kernel(x_ref, g_ref, w_ref, *out_refs):
    h = _rms(x_ref[...], g_ref[...]).astype(BF16)
    dest = [(ref, a) for ref, (_, width) in zip(out_refs, MIX_SLABS) for a in range(0, width, LANES)]
    for c in range(0, MIX_COLS, MXU_N):
        width = min(MXU_N, MIX_COLS - c)
        y = jnp.dot(h, w_ref[:, c:c + width], preferred_element_type=F32)
        for k in range(width // LANES):
            ref, a = dest[c // LANES + k]
            ref[:, a:a + LANES] = y[:, k * LANES:(k + 1) * LANES]


def _mix_in_call(x, norm_g, w_packed):
    n = x.shape[0]
    tm = min(512, n)
    row = lambda i: (i, 0)
    const = lambda i: (0, 0)
    return pl.pallas_call(
        _mix_in_kernel,
        grid=(n // tm,),
        in_specs=[pl.BlockSpec((tm, D_MODEL), row), pl.BlockSpec((1, D_MODEL), const),
                  pl.BlockSpec((D_MODEL, MIX_COLS), const)],
        out_specs=[pl.BlockSpec((tm, w), row) for _, w in MIX_SLABS],
        out_shape=[jax.ShapeDtypeStruct((n, w), F32) for _, w in MIX_SLABS],
        compiler_params=_cp(("parallel",)),
        name="mix_in",
    )(x, norm_g, w_packed)


def _shift_lerp(p, mu, carry_ref, first_tile):
    rows = lax.broadcasted_iota(I32, p.shape, 0)
    carry = jnp.where(first_tile, jnp.zeros_like(carry_ref[0:1, :]), carry_ref[0:1, :])
    prev = jnp.where(rows == 0, carry, pltpu.roll(p, 1, axis=0))
    carry_ref[0:1, :] = p[p.shape[0] - 1:, :]
    return p + (prev - p) * mu


def _rwkv_prep_kernel(*refs, has_vres):
    it = iter(refs)
    rkv_ref, lm_ref = next(it), next(it)
    vfirst_ref = next(it) if has_vres else None
    (mu_rkv_ref, mu_lm_ref, w0_ref, a0_ref, wl_ref, kk_ref, ka_ref, v0_ref, v2_ref,
     ones_ref) = (next(it) for _ in range(10))
    r_ref, kp_ref, v_ref, nk_ref, beta_ref, lw_ref, g_ref = (next(it) for _ in range(7))
    c_rkv_ref, c_lm_ref = next(it), next(it)

    first = pl.program_id(1) == 0
    xa = _shift_lerp(rkv_ref[...], mu_rkv_ref[...], c_rkv_ref, first)
    xl = _shift_lerp(lm_ref[...], mu_lm_ref[...], c_lm_ref, first)
    r = xa[:, 0:A_WIDTH]
    k = xa[:, A_WIDTH:2 * A_WIDTH]
    v = xa[:, 2 * A_WIDTH:3 * A_WIDTH]
    lora = xl[:, 0:LANES]
    lane = lax.broadcasted_iota(I32, lora.shape, 1)
    feat = jnp.where(lane < 32, jnp.tanh(lora), jnp.where(lane < 64, lora, _sigmoid(lora)))
    proj = _dot(feat, wl_ref[...])
    w_pre = w0_ref[...] + proj[:, 0:A_WIDTH]
    sp = jnp.maximum(-w_pre, 0.0) + jnp.log(1.0 + jnp.exp(-jnp.abs(w_pre)))
    lw_ref[...] = -jnp.exp(-sp - 0.5)
    eta = _sigmoid(a0_ref[...] + proj[:, A_WIDTH:2 * A_WIDTH])
    g_ref[...] = proj[:, 2 * A_WIDTH:3 * A_WIDTH]
    kk = k * kk_ref[...]
    ss = _dot_hl_r(kk * kk, ones_ref[...])
    nk = kk / jnp.maximum(jnp.sqrt(ss), 1e-12)
    nk_ref[...] = nk
    beta_ref[...] = nk * eta
    kp_ref[...] = k * (1.0 + (eta - 1.0) * ka_ref[...])
    r_ref[...] = r
    if has_vres:
        mv = xl[:, LANES:2 * LANES]
        mix = _sigmoid(v0_ref[...] + _dot(mv, v2_ref[...]))
        v = v + (vfirst_ref[...] - v) * mix
    v_ref[...] = v


def _rwkv_prep_call(rkv, lm, vfirst, pw, bsz, t):
    tt = min(256, t)
    nt = t // tt
    tok = lambda b, i: (b * nt + i, 0)
    const = lambda b, i: (0, 0)
    has_vres = vfirst is not None
    in_specs = [pl.BlockSpec((tt, 1152), tok), pl.BlockSpec((tt, 256), tok)]
    args = [rkv, lm]
    if has_vres:
        in_specs.append(pl.BlockSpec((tt, A_WIDTH), tok))
        args.append(vfirst)
    params = [pw["mu_rkv"], pw["mu_lm"], pw["w0"], pw["a0"], pw["w_lora"], pw["k_k"], pw["k_a"],
              pw["v0"], pw["v2"], pw["ones_a"]]
    in_specs += [pl.BlockSpec(p.shape, const) for p in params]
    args += params
    n = bsz * t
    out = jax.ShapeDtypeStruct((n, A_WIDTH), F32)
    return pl.pallas_call(
        functools.partial(_rwkv_prep_kernel, has_vres=has_vres),
        grid=(bsz, nt),
        in_specs=in_specs,
        out_specs=[pl.BlockSpec((tt, A_WIDTH), tok)] * 7,
        out_shape=[out] * 7,
        scratch_shapes=[pltpu.VMEM((8, 1152), F32), pltpu.VMEM((8, 256), F32)],
        compiler_params=_cp(("parallel", "arbitrary")),
        name="rwkv_prep",
    )(*args)


def _bd(x, bd_mask):
    return jnp.where(bd_mask, jnp.concatenate([x, x], axis=0), 0.0)


def _chunk_masks():
    r = lax.broadcasted_iota(I32, (PAIR, PAIR), 0)
    c = lax.broadcasted_iota(I32, (PAIR, PAIR), 1)
    same = (r >> 6) == (c >> 6)
    return r, c, same


def _rwkv_core_kernel(r_ref, kp_ref, v_ref, nk_ref, beta_ref, lw_ref, g_ref,
                      lnw_ref, lnb_ref, rk_ref, tri_ref, lvl_ref, ones_ref,
                      o_ref, s_ref):
    @pl.when(pl.program_id(1) == 0)
    def _():
        s_ref[...] = jnp.zeros_like(s_ref)

    row, col, same = _chunk_masks()
    rl, cl = row & (CHUNK - 1), col & (CHUNK - 1)
    strict = same & (rl > cl)
    incl = same & (rl >= cl)
    eye = (row == col).astype(F32)
    tri = tri_ref[...]
    ones_pair = ones_ref[...]
    n_chunks = r_ref.shape[0] // CHUNK

    n_pairs = A_WIDTH // PAIR
    inst = [(ci, p) for ci in range(RWKV_CPI) for p in range(n_pairs)]
    each = lambda f, *cols: [f(*a) for a in zip(*cols)]

    def chunk_body(it, carry):
        rows = [pl.ds(pl.multiple_of((it * RWKV_CPI + ci) * CHUNK, CHUNK), CHUNK) for ci, _ in inst]
        sls = [slice(p * PAIR, (p + 1) * PAIR) for _, p in inst]
        ld = lambda ref: [ref[rw, sl] for rw, sl in zip(rows, sls)]
        r, kp, v, nk, beta, lw = (ld(ref) for ref in (r_ref, kp_ref, v_ref, nk_ref, beta_ref, lw_ref))
        cum = each(lambda x: _dot_hl(tri, x), lw)
        cum_l = each(lambda c: c[CHUNK - 1:CHUNK, :], cum)
        a_bd = each(lambda n, c, w: _bd(-n * jnp.exp(c - w), same), nk, cum, lw)
        r_bd = each(lambda x, c: _bd(x * jnp.exp(c), same), r, cum)
        g_inv = each(lambda c: jnp.exp(-c), cum)
        g_tail = each(lambda c, cl_: jnp.exp(cl_ - c), cum, cum_l)
        b_bd = each(lambda x, g: _bd(x * g, same), beta, g_inv)
        k_bd = each(lambda x, g: _bd(x * g, same), kp, g_inv)
        v_bd = each(lambda x: _bd(x, same), v)
        bg_bd = each(lambda x, g: _bd(x * g, same), beta, g_tail)
        kg_bd = each(lambda x, g: _bd(x * g, same), kp, g_tail)

        ar = each(lambda a, b: jnp.concatenate([a, b], axis=0), a_bd, r_bd)
        bk = each(lambda a, b: jnp.concatenate([a, b], axis=0), b_bd, k_bd)
        gram = each(_dot_nt, ar, bk)
        a_ab = each(lambda g: jnp.where(strict, g[:PAIR, :PAIR], 0.0), gram)
        a_ak = each(lambda g: jnp.where(strict, g[:PAIR, PAIR:], 0.0), gram)
        a_rb = each(lambda g: jnp.where(incl, g[PAIR:, :PAIR], 0.0), gram)
        a_rk = each(lambda g: jnp.where(incl, g[PAIR:, PAIR:], 0.0), gram)

        t_inv = each(lambda a: eye + a * lvl_ref[0], a_ab)
        w_loc = each(_dot, a_ak, v_bd)
        rkv = each(_dot, a_rk, v_bd)
        n_loc = each(_dot_tn, v_bd, kg_bd)
        for lvl in range(1, 6):
            x = each(lambda a, t_: _dot(a * lvl_ref[lvl], t_), a_ab, t_inv)
            t_inv = each(lambda t_, x_: t_ + _dot(t_, x_), t_inv, x)

        au = each(lambda t_, a, w: _dot(t_, jnp.concatenate([a, w], axis=1)), t_inv, a_bd, w_loc)
        ry = each(lambda rb, y0, arb, au_: jnp.concatenate([rb, y0], axis=1) + _dot(arb, au_),
                  r_bd, rkv, a_rb, au)
        mn = each(_dot_tn, au, bg_bd)

        y = [None] * len(inst)
        for ci in range(RWKV_CPI):
            idx = [i for i, (cj, _) in enumerate(inst) if cj == ci]
            s = [s_ref[inst[i][1]] for i in idx]
            y_bd = [_dot_nt(ry[i][:, :PAIR], s_) + ry[i][:, PAIR:] for i, s_ in zip(idx, s)]
            s_new = [s_ * jnp.exp(cum_l[i]) + _dot(s_, mn[i][:PAIR]) + mn[i][PAIR:] + n_loc[i]
                     for i, s_ in zip(idx, s)]
            for i, sn, yb in zip(idx, s_new, y_bd):
                s_ref[inst[i][1]] = sn
                y[i] = yb[:CHUNK] + yb[CHUNK:]

        sums = each(lambda y_, r_, k_, sl: _dot_hl_r(
            jnp.concatenate([y_, r_ * k_ * rk_ref[:, sl]], axis=0), ones_pair), y, r, kp, sls)
        yc = each(lambda y_, s_: y_ - s_[:CHUNK] * (1.0 / HEAD_DIM), y, sums)
        var = each(lambda c: _dot_hl_r(c * c, ones_pair) * (1.0 / HEAD_DIM), yc)
        for i, (rw, sl) in enumerate(zip(rows, sls)):
            yn = yc[i] * lax.rsqrt(var[i] + A_GN_EPS) * lnw_ref[:, sl] + lnb_ref[:, sl]
            o_ref[rw, sl] = ((yn + sums[i][CHUNK:] * v[i]) * g_ref[rw, sl]).astype(o_ref.dtype)
        return carry

    lax.fori_loop(0, n_chunks // RWKV_CPI, chunk_body, 0)


def _rwkv_core_call(coef, pw, consts, bsz, t):
    tt = min(512, t)
    nt = t // tt
    tok = lambda b, i: (b * nt + i, 0)
    c2 = lambda b, i: (0, 0)
    c3 = lambda b, i: (0, 0, 0)
    params = [pw["ln_w"], pw["ln_b"], pw["r_k"]]
    return pl.pallas_call(
        _rwkv_core_kernel,
        grid=(bsz, nt),
        in_specs=([pl.BlockSpec((tt, A_WIDTH), tok)] * 7
                  + [pl.BlockSpec((1, A_WIDTH), c2)] * 3
                  + [pl.BlockSpec((CHUNK, CHUNK), c2), pl.BlockSpec((6, PAIR, PAIR), c3),
                     pl.BlockSpec((PAIR, PAIR), c2)]),
        out_specs=pl.BlockSpec((tt, A_WIDTH), tok),
        out_shape=jax.ShapeDtypeStruct((bsz * t, A_WIDTH), BF16),
        scratch_shapes=[pltpu.VMEM((A_WIDTH // PAIR, PAIR, PAIR), F32)],
        compiler_params=_cp(("parallel", "arbitrary")),
        name="rwkv_core",
    )(*coef, *params, consts["tri"], consts["levels"], consts["ones_pair"])


def _swap_halves(z):
    lane = lax.broadcasted_iota(I32, z.shape, 1)
    return jnp.where((lane & (HEAD_DIM // 2)) == 0,
                     pltpu.roll(z, LANES - HEAD_DIM // 2, axis=1),
                     pltpu.roll(z, HEAD_DIM // 2, axis=1))


def _ret_kernel(pc_ref, cos_ref, sin_ref, dmask_ref, qdec_ref, kdec_ref, cdec_ref, ones_ref,
                o_ref, z_ref):
    @pl.when(pl.program_id(1) == 0)
    def _():
        z_ref[...] = jnp.zeros_like(z_ref)

    _, _, same = _chunk_masks()
    ones_pair = ones_ref[...]
    n_chunks = pc_ref.shape[0] // CHUNK

    n_pairs = C_WIDTH // PAIR
    inst = [(ci, p) for ci in range(RET_CPI) for p in range(n_pairs)]
    each = lambda f, *cols: [f(*a) for a in zip(*cols)]

    def chunk_body(it, carry):
        rows = [pl.ds(pl.multiple_of((it * RET_CPI + ci) * CHUNK, CHUNK), CHUNK) for ci, _ in inst]
        sls = [slice(p * PAIR, (p + 1) * PAIR) for _, p in inst]
        cos = [cos_ref[rw, sl] for rw, sl in zip(rows, sls)]
        sin = [sin_ref[rw, sl] for rw, sl in zip(rows, sls)]
        part = lambda j: [pc_ref[rw, j * C_WIDTH + p * PAIR:j * C_WIDTH + (p + 1) * PAIR]
                          for rw, (_, p) in zip(rows, inst)]
        q, k, v = part(0), part(1), part(2)
        qr = each(lambda z, c, s: z * c + _swap_halves(z) * s, q, cos, sin)
        kr = each(lambda z, c, s: (z * c + _swap_halves(z) * s) * (HEAD_DIM ** -0.5), k, cos, sin)
        q_bd = each(lambda z: _bd(z, same), qr)
        k_bd = each(lambda z: _bd(z, same), kr)
        v_bd = each(lambda z: _bd(z, same), v)
        qd_bd = each(lambda z, sl: _bd(z * qdec_ref[:, sl], same), qr, sls)
        kd_bd = each(lambda z, sl: _bd(z * kdec_ref[:, sl], same), kr, sls)
        scores = each(lambda a, b, ip: _dot_nt(a, b) * dmask_ref[ip[1]], q_bd, k_bd, inst)
        inner = each(_dot, scores, v_bd)
        kv = each(_dot_tn, kd_bd, v_bd)
        y = [None] * len(inst)
        for ci in range(RET_CPI):
            idx = [i for i, (cj, _) in enumerate(inst) if cj == ci]
            z = [z_ref[inst[i][1]] for i in idx]
            cross = [_dot(qd_bd[i], z_) for i, z_ in zip(idx, z)]
            for i, z_, cr in zip(idx, z, cross):
                z_ref[inst[i][1]] = z_ * cdec_ref[:, sls[i]] + kv[i]
                yb = inner[i] + cr
                y[i] = yb[:CHUNK] + yb[CHUNK:]
        ms = each(lambda y_: _dot_hl_r(y_ * y_, ones_pair) * (1.0 / HEAD_DIM), y)
        g = part(3)
        for i, (rw, sl) in enumerate(zip(rows, sls)):
            o_ref[rw, sl] = (g[i] * _sigmoid(g[i]) * y[i] * lax.rsqrt(ms[i] + NORM_EPS)).astype(o_ref.dtype)
        return carry

    lax.fori_loop(0, n_chunks // RET_CPI, chunk_body, 0)


def _ret_call(pc, consts, bsz, t):
    tt = min(512, t)
    nt = t // tt
    tok = lambda b, i: (b * nt + i, 0)
    c2 = lambda b, i: (0, 0)
    return pl.pallas_call(
        _ret_kernel,
        grid=(bsz, nt),
        in_specs=[pl.BlockSpec((tt, 4 * C_WIDTH), tok),
                  pl.BlockSpec((tt, C_WIDTH), lambda b, i: (i, 0)),
                  pl.BlockSpec((tt, C_WIDTH), lambda b, i: (i, 0)),
                  pl.BlockSpec((C_WIDTH // PAIR, PAIR, PAIR), lambda b, i: (0, 0, 0)),
                  pl.BlockSpec((CHUNK, C_WIDTH), c2), pl.BlockSpec((CHUNK, C_WIDTH), c2),
                  pl.BlockSpec((1, C_WIDTH), c2), pl.BlockSpec((PAIR, PAIR), c2)],
        out_specs=pl.BlockSpec((tt, C_WIDTH), tok),
        out_shape=jax.ShapeDtypeStruct((bsz * t, C_WIDTH), BF16),
        scratch_shapes=[pltpu.VMEM((C_WIDTH // PAIR, PAIR, PAIR), F32)],
        compiler_params=_cp(("parallel", "arbitrary")),
        name="retention",
    )(pc, consts["cos"], consts["sin"], consts["ret_dmask"], consts["ret_qdec"],
      consts["ret_kdec"], consts["ret_cdec"], consts["ones_pair"])


def _dsa_kernel(bq_ref, iq_ref, misc_ref, kvn_ref, wuk_ref, wuvt_ref, tri_ref,
                o_ref, c_ref, ct_ref, key_ref, logit_ref, rhs_ref, qls_ref, acc_ref, *, topk):
    j = pl.program_id(1)
    t = misc_ref.shape[0]
    n_units = j + 1
    n_heads = B_HEADS

    @pl.when(j == 0)
    def _():
        def body(i, carry):
            rows = pl.ds(pl.multiple_of(i * QBLK, QBLK), QBLK)
            c = _rms(misc_ref[rows, 0:KV_LATENT], kvn_ref[...])
            c_ref[rows, :] = c.astype(BF16)
            ct_ref[:, rows] = c.T.astype(BF16)
            return carry
        lax.fori_loop(0, t // QBLK, body, 0)

    qrows = pl.ds(pl.multiple_of(j * QB, QB), QB)
    qlat = _dot(bq_ref[...], wuk_ref[...]) * (HEAD_DIM ** -0.5)
    for h in range(n_heads):
        qls_ref[h * QB:(h + 1) * QB, :] = qlat[:, h * LANES:(h + 1) * LANES].astype(BF16)
    iq = iq_ref[...] * (HEAD_DIM ** -0.5)
    lane = lax.broadcasted_iota(I32, (QB, LANES), 1)
    for hp in range(IDX_HEADS // 2):
        slab = iq[:, hp * LANES:(hp + 1) * LANES]
        rhs_ref[hp, 0:QB, :] = jnp.where(lane < HEAD_DIM, slab, 0.0).astype(BF16)
        rhs_ref[hp, QB:2 * QB, :] = jnp.where(lane >= HEAD_DIM, slab, 0.0).astype(BF16)
    iw_t = misc_ref[qrows, 2 * LANES:3 * LANES].T * (IDX_HEADS ** -0.5)

    srow = lax.broadcasted_iota(I32, (QBLK, QB), 0)
    qcol = lax.broadcasted_iota(I32, (QBLK, QB), 1)
    qpos = j * QB + qcol

    def visible(kb):
        return (kb * QBLK + srow) <= qpos

    halves = lambda u: [2 * u, 2 * u + 1]
    blk_rows = lambda kb: pl.ds(pl.multiple_of(kb * QBLK, QBLK), QBLK)
    unit_rows = lambda u: pl.ds(pl.multiple_of(u * 2 * QBLK, 2 * QBLK), 2 * QBLK)

    def score_body(u, carry):
        kbs = halves(u)
        ik2 = [misc_ref[blk_rows(kb), LANES:2 * LANES].astype(BF16) for kb in kbs]
        acc = [jnp.zeros((QBLK, QB), F32) for _ in kbs]
        for hp in range(IDX_HEADS // 2):
            rel = [lax.dot_general(x, rhs_ref[hp], (((1,), (1,)), ((), ())),
                                   preferred_element_type=F32) for x in ik2]
            acc = [a + iw_t[2 * hp:2 * hp + 1, :] * jnp.maximum(r_[:, :QB], 0.0)
                   + iw_t[2 * hp + 1:2 * hp + 2, :] * jnp.maximum(r_[:, QB:], 0.0)
                   for a, r_ in zip(acc, rel)]
        for kb, a in zip(kbs, acc):
            score = jnp.where(visible(kb), a + 0.0, -jnp.inf)
            bits = pltpu.bitcast(score, I32)
            key = jnp.where(bits < 0, bits ^ 0x7FFFFFFF, bits)
            key_ref[blk_rows(kb), :] = key
        return carry
    lax.fori_loop(0, n_units, score_body, 0)

    def count(pred):
        def body(u, acc):
            hit = jnp.where(pred(key_ref[unit_rows(u), :]), 1, 0)
            return acc + jnp.sum(hit.reshape(2 * QBLK // 8, 8, QB), axis=0)
        acc = lax.fori_loop(0, n_units, body, jnp.zeros((8, QB), I32))
        return jnp.sum(acc, axis=0, keepdims=True)

    c0 = count(lambda k: k >= 0)
    tau0 = jnp.where(c0 >= topk, 0, INT_MIN).astype(I32)

    def bit_body(i, tau):
        cand = tau | jnp.left_shift(jnp.int32(1), 30 - i)
        cnt = count(lambda k: k >= cand)
        return jnp.where(cnt >= topk, cand, tau)
    tau = lax.fori_loop(0, 31, bit_body, tau0)
    need = (topk - count(lambda k: k > tau)).astype(F32)

    tri = tri_ref[...]

    def logit_body(u, carry):
        seen, m = carry
        kbs = halves(u)
        keys = [key_ref[blk_rows(kb), :] for kb in kbs]
        eqs = [k == tau for k in keys]
        pref = [jnp.dot(tri, jnp.where(e, 1.0, 0.0).astype(BF16), preferred_element_type=F32)
                for e in eqs]
        logits = lax.dot_general(c_ref[unit_rows(u), :], qls_ref[...], (((1,), (1,)), ((), ())),
                                 preferred_element_type=F32)
        for i, kb in enumerate(kbs):
            rank = seen + pref[i]
            seen = rank[QBLK - 1:QBLK, :]
            sel = visible(kb) & ((keys[i] > tau) | (eqs[i] & (rank <= need)))
            new_m = []
            for h in range(n_heads):
                lh = jnp.where(sel, logits[i * QBLK:(i + 1) * QBLK, h * QB:(h + 1) * QB], NEG_BIG)
                logit_ref[blk_rows(kb), h * QB:(h + 1) * QB] = lh
                new_m.append(jnp.max(lh, axis=0, keepdims=True))
            m = jnp.maximum(m, jnp.concatenate(new_m, axis=1))
        return seen, m

    _, m = lax.fori_loop(0, n_units, logit_body,
                         (jnp.zeros((1, QB), F32), jnp.full((1, n_heads * QB), NEG_BIG, F32)))

    acc_ref[...] = jnp.zeros_like(acc_ref)

    def pv_body(u, lsum):
        rows = unit_rows(u)
        p = jnp.exp(logit_ref[rows, :] - m)
        acc_ref[...] += jnp.dot(ct_ref[:, rows], p.astype(BF16), preferred_element_type=F32)
        return lsum + jnp.sum(p, axis=0, keepdims=True)
    lsum = lax.fori_loop(0, n_units, pv_body, jnp.zeros((1, n_heads * QB), F32))

    o_lat = acc_ref[...] / lsum
    outs = [jnp.dot(wuvt_ref[h], o_lat[:, h * QB:(h + 1) * QB].astype(BF16),
                    preferred_element_type=F32) for h in range(n_heads)]
    o_ref[...] = jnp.concatenate(outs, axis=0).T.astype(o_ref.dtype)


def _dsa_call(bq, iq, misc, dw, consts, bsz, t):
    nq = t // QB
    topk = min(TOPK_MAX, t // 4)
    blk = lambda b, j: (b * nq + j, 0)
    c2 = lambda b, j: (0, 0)
    return pl.pallas_call(
        functools.partial(_dsa_kernel, topk=topk),
        grid=(bsz, nq),
        in_specs=[pl.BlockSpec((QB, B_WIDTH), blk), pl.BlockSpec((QB, 512), blk),
                  pl.BlockSpec((t, 384), lambda b, j: (b, 0)),
                  pl.BlockSpec((1, KV_LATENT), c2),
                  pl.BlockSpec((B_WIDTH, B_HEADS * KV_LATENT), c2),
                  pl.BlockSpec((B_HEADS, HEAD_DIM, KV_LATENT), lambda b, j: (0, 0, 0)),
                  pl.BlockSpec((QBLK, QBLK), c2)],
        out_specs=pl.BlockSpec((QB, B_WIDTH), blk),
        out_shape=jax.ShapeDtypeStruct((bsz * t, B_WIDTH), BF16),
        scratch_shapes=[pltpu.VMEM((t, KV_LATENT), BF16), pltpu.VMEM((KV_LATENT, t), BF16),
                        pltpu.VMEM((t, QB), I32), pltpu.VMEM((t, B_HEADS * QB), F32),
                        pltpu.VMEM((IDX_HEADS // 2, 2 * QB, LANES), BF16),
                        pltpu.VMEM((B_HEADS * QB, KV_LATENT), BF16),
                        pltpu.VMEM((KV_LATENT, B_HEADS * QB), F32)],
        compiler_params=_cp(("parallel", "arbitrary")),
        name="dsa",
    )(bq, iq, misc, dw["kv_norm"], dw["wuk_bd"], dw["wuv_t"], consts["tri128"])


def _pad_cols(w, width):
    return jnp.pad(w, ((0, 0), (0, width - w.shape[1])))


def _pack_w_in(w_in, vres_w_in):
    a, b, c = w_in[:, :1280], w_in[:, 1280:2376], w_in[:, 2376:3400]
    mv = (jnp.zeros((D_MODEL, LANES), F32) if vres_w_in is None else _pad_cols(vres_w_in, LANES))
    ckv, ik, iw = b[:, 384:512], b[:, 1024:1088], b[:, 1088:1096]
    misc = jnp.concatenate([ckv, ik, ik, _pad_cols(iw, LANES)], axis=1)
    packed = jnp.concatenate([a[:, :1152], a[:, 1152:1280], mv, b[:, :384], b[:, 512:1024], misc, c],
                             axis=1)
    return packed.astype(BF16)


def _row(v):
    return v.reshape(1, -1).astype(F32)


def _block_diag(blocks):
    n = len(blocks)
    r, c = blocks[0].shape
    out = jnp.zeros((n * r, n * c), blocks[0].dtype)
    for i, blk in enumerate(blocks):
        out = out.at[i * r:(i + 1) * r, i * c:(i + 1) * c].set(blk)
    return out


def _constants(t):
    consts = {}
    i64 = jnp.arange(CHUNK)
    consts["tri"] = (i64[:, None] >= i64[None, :]).astype(BF16)
    i128 = jnp.arange(PAIR)
    consts["tri128"] = (i128[:, None] >= i128[None, :]).astype(BF16)
    consts["ones_pair"] = ((i128[:, None] // HEAD_DIM) == (i128[None, :] // HEAD_DIM)).astype(BF16)
    i384 = jnp.arange(A_WIDTH)
    consts["ones_a"] = ((i384[:, None] // HEAD_DIM) == (i384[None, :] // HEAD_DIM)).astype(BF16)
    r, c = i128[:, None], i128[None, :]
    lv = []
    for lvl in range(6):
        s = 1 << lvl
        lv.append(((r // (2 * s)) == (c // (2 * s))) & ((r % (2 * s)) >= s) & ((c % (2 * s)) < s))
    consts["levels"] = jnp.stack(lv).astype(F32)
    half = HEAD_DIM // 2
    theta = ROPE_BASE ** (-jnp.linspace(0.0, 1.0, half, dtype=F32))
    ang = jnp.arange(t, dtype=F32)[:, None] * theta[None, :]
    cos, sin = jnp.cos(ang), jnp.sin(ang)
    n_heads = C_WIDTH // HEAD_DIM
    consts["cos"] = jnp.tile(jnp.concatenate([cos, cos], axis=1), (1, n_heads))
    consts["sin"] = jnp.tile(jnp.concatenate([-sin, sin], axis=1), (1, n_heads))
    log_gamma = jnp.log(1.0 - 2.0 ** (-5.0 - jnp.arange(n_heads, dtype=F32)))
    n = jnp.arange(CHUNK, dtype=F32)
    diff = n[:, None] - n[None, :]
    intra = jnp.where(diff[None] >= 0,
                      jnp.exp(jnp.maximum(diff, 0.0)[None] * log_gamma[:, None, None]), 0.0)
    consts["ret_dmask"] = jnp.stack([_block_diag([intra[2 * p], intra[2 * p + 1]])
                                     for p in range(n_heads // 2)])
    lane_head = jnp.repeat(jnp.arange(n_heads), HEAD_DIM)
    consts["ret_qdec"] = jnp.exp((n[:, None] + 1.0) * log_gamma[None, :])[:, lane_head]
    consts["ret_kdec"] = jnp.exp((CHUNK - 1.0 - n)[:, None] * log_gamma[None, :])[:, lane_head]
    consts["ret_cdec"] = jnp.exp(CHUNK * log_gamma)[None, lane_head]
    return consts


def _pad_ff(wg, wu, wd):
    pad = D_FF_PAD - D_FF
    return (jnp.pad(wg.astype(BF16), ((0, 0), (0, 0), (0, pad))),
            jnp.pad(wu.astype(BF16), ((0, 0), (0, 0), (0, pad))),
            jnp.pad(wd.astype(BF16), ((0, 0), (0, pad), (0, 0))))


def kernel(x, ffn1_norm, ffn1_w_gate, ffn1_w_up, ffn1_w_down, mix_norm, w_in, w_out, rwkv_mu, rwkv_w0, rwkv_w2, rwkv_a0, rwkv_a2, rwkv_g2, rwkv_k_k, rwkv_k_a, rwkv_r_k, rwkv_ln_w, rwkv_ln_b, rwkv_vres_w_in, rwkv_vres_mu, rwkv_v0, rwkv_v2, dsa_kv_norm, dsa_w_uk, dsa_w_uv, ffn2_norm, ffn2_w_gate, ffn2_w_up, ffn2_w_down, final_norm):
    bsz, t, _ = x.shape
    depth = w_in.shape[0]
    consts = _constants(t)
    xf = x.reshape(bsz * t, D_MODEL)
    ffn1_w = _pad_ff(ffn1_w_gate, ffn1_w_up, ffn1_w_down)
    ffn2_w = _pad_ff(ffn2_w_gate, ffn2_w_up, ffn2_w_down)
    v_first = None
    for l in range(depth):
        xf = _ffn_call(xf, _row(ffn1_norm[l]), ffn1_w, l)
        w_packed = _pack_w_in(w_in[l], None if l == 0 else rwkv_vres_w_in[l - 1])
        rkv, lm, bq, iq, misc, pc = _mix_in_call(xf, _row(mix_norm[l]), w_packed)

        zeros_a = jnp.zeros((1, A_WIDTH), F32)
        w_lora = jnp.zeros((LANES, 3 * A_WIDTH), F32)
        w_lora = w_lora.at[0:32, 0:A_WIDTH].set(rwkv_w2[l])
        w_lora = w_lora.at[32:64, A_WIDTH:2 * A_WIDTH].set(rwkv_a2[l])
        w_lora = w_lora.at[64:128, 2 * A_WIDTH:].set(rwkv_g2[l])
        mu = rwkv_mu[l]
        mu_mv = jnp.zeros((LANES,), F32) if l == 0 else jnp.pad(rwkv_vres_mu[l - 1], (0, LANES - 16))
        pw = {
            "mu_rkv": _row(mu[:1152]),
            "mu_lm": _row(jnp.concatenate([mu[1152:1280], mu_mv])),
            "w0": _row(rwkv_w0[l]), "a0": _row(rwkv_a0[l]), "w_lora": w_lora.astype(BF16),
            "k_k": _row(rwkv_k_k[l]), "k_a": _row(rwkv_k_a[l]),
            "v0": zeros_a if l == 0 else _row(rwkv_v0[l - 1]),
            "v2": (jnp.zeros((LANES, A_WIDTH), BF16) if l == 0
                   else jnp.pad(rwkv_v2[l - 1], ((0, LANES - 16), (0, 0))).astype(BF16)),
            "ones_a": consts["ones_a"],
            "ln_w": _row(rwkv_ln_w[l]), "ln_b": _row(rwkv_ln_b[l]), "r_k": _row(rwkv_r_k[l]),
        }
        coef = _rwkv_prep_call(rkv, lm, v_first, pw, bsz, t)
        if l == 0:
            v_first = coef[2]
        o_a = _rwkv_core_call(coef, pw, consts, bsz, t)

        dw = {
            "kv_norm": _row(dsa_kv_norm[l]),
            "wuk_bd": _block_diag([dsa_w_uk[l, h] for h in range(B_HEADS)]).astype(BF16),
            "wuv_t": jnp.swapaxes(dsa_w_uv[l], 1, 2).astype(BF16),
        }
        o_b = _dsa_call(bq, iq, misc, dw, consts, bsz, t)

        o_c = _ret_call(pc, consts, bsz, t)

        wo = w_out[l].astype(BF16)
        mix = (o_a, o_b, o_c, wo[:A_WIDTH], wo[A_WIDTH:A_WIDTH + B_WIDTH], wo[A_WIDTH + B_WIDTH:])
        xf = _ffn_call(xf, _row(ffn2_norm[l]), ffn2_w, l,
                       mix=mix, final_g=_row(final_norm) if l == depth - 1 else None)
    return xf.reshape(bsz, t, D_MODEL)
```

```python
import functools
import math

import jax
import jax.numpy as jnp
from jax import lax
from jax.experimental import pallas as pl
from jax.experimental.pallas import tpu as pltpu

F32 = jnp.float32
BF16 = jnp.bfloat16
I32 = jnp.int32

LANES = 128
MXU_N = 256
HEAD_DIM = 64
PAIR = 2 * HEAD_DIM
CHUNK = 64
RWKV_CPI = 4
RET_CPI = 4
NORM_EPS = 1e-6
A_GN_EPS = 64e-5
D_MODEL = 1024
D_FF = 2752
FF_TILE = 256
A_WIDTH = 384
B_WIDTH = 384
C_WIDTH = 256
KV_LATENT = 128
IDX_HEADS = 8
B_HEADS = 6
TOPK_MAX = 256
QBLK = 128
QB = 256
ROPE_BASE = 10000.0
NEG_BIG = -1e30
INT_MIN = -2147483648
VMEM_LIMIT = 52 * 1024 * 1024


def _cp(sem):
    return pltpu.CompilerParams(dimension_semantics=sem, vmem_limit_bytes=VMEM_LIMIT)


def _dot(a, b):
    return jnp.dot(a.astype(BF16), b.astype(BF16), preferred_element_type=F32)


def _dot_nt(a, b):
    return lax.dot_general(a.astype(BF16), b.astype(BF16), (((1,), (1,)), ((), ())),
                           preferred_element_type=F32)


def _dot_tn(a, b):
    return lax.dot_general(a.astype(BF16), b.astype(BF16), (((0,), (0,)), ((), ())),
                           preferred_element_type=F32)


def _dot_hl(m01, x):
    hi = x.astype(BF16)
    lo = (x - hi.astype(F32)).astype(BF16)
    n = x.shape[1]
    both = jnp.dot(m01, jnp.concatenate([hi, lo], axis=1), preferred_element_type=F32)
    return both[:, :n] + both[:, n:]


def _dot_hl_r(x, m01):
    hi = x.astype(BF16)
    lo = (x - hi.astype(F32)).astype(BF16)
    m = x.shape[0]
    both = jnp.dot(jnp.concatenate([hi, lo], axis=0), m01, preferred_element_type=F32)
    return both[:m] + both[m:]


def _rms(x, g):
    return x * lax.rsqrt(jnp.mean(x * x, axis=-1, keepdims=True) + NORM_EPS) * g


def _sigmoid(x):
    return 1.0 / (1.0 + jnp.exp(-x))


def _ffn_kernel(*refs, has_mix, final):
    it = iter(refs)
    x_ref = next(it)
    if has_mix:
        oa_ref, ob_ref, oc_ref, woa_ref, wob_ref, woc_ref = (next(it) for _ in range(6))
    g_ref, wg_ref, wu_ref, wd_ref = (next(it) for _ in range(4))
    fg_ref = next(it) if final else None
    out_ref = next(it)

    x = x_ref[...]
    if has_mix:
        x = (x + _dot(oa_ref[...], woa_ref[...]) + _dot(ob_ref[...], wob_ref[...])
             + _dot(oc_ref[...], woc_ref[...]))
    h = _rms(x, g_ref[...]).astype(BF16)
    for c in range(0, D_FF, FF_TILE):
        w = min(FF_TILE, D_FF - c)
        gate = jnp.dot(h, wg_ref[:, c:c + w].astype(BF16), preferred_element_type=F32)
        up = jnp.dot(h, wu_ref[:, c:c + w].astype(BF16), preferred_element_type=F32)
        act = (gate * _sigmoid(gate) * up).astype(BF16)
        part = jnp.dot(act, wd_ref[c:c + w, :].astype(BF16), preferred_element_type=F32)
        if c == 0:
            out_ref[...] = part
        else:
            out_ref[...] += part
    y = x + 0.5 * out_ref[...]
    if final:
        y = _rms(y, fg_ref[...])
    out_ref[...] = y


def _ffn_call(x, norm_g, weights, layer, mix=None, final_g=None):
    wg, wu, wd = weights
    n = x.shape[0]
    tm = min(512, n)
    row = lambda i: (i, 0)
    const = lambda i: (0, 0)
    whole = lambda i: (layer, 0, 0)
    resident = pl.Buffered(1)
    in_specs = [pl.BlockSpec((tm, D_MODEL), row)]
    args = [x]
    if mix is not None:
        oa, ob, oc, woa, wob, woc = mix
        in_specs += [pl.BlockSpec((tm, A_WIDTH), row), pl.BlockSpec((tm, B_WIDTH), row),
                     pl.BlockSpec((tm, C_WIDTH), row),
                     pl.BlockSpec((A_WIDTH, D_MODEL), const), pl.BlockSpec((B_WIDTH, D_MODEL), const),
                     pl.BlockSpec((C_WIDTH, D_MODEL), const)]
        args += [oa, ob, oc, woa, wob, woc]
    in_specs += [pl.BlockSpec((1, D_MODEL), const),
                 pl.BlockSpec((None, D_MODEL, D_FF), whole, pipeline_mode=resident),
                 pl.BlockSpec((None, D_MODEL, D_FF), whole, pipeline_mode=resident),
                 pl.BlockSpec((None, D_FF, D_MODEL), whole, pipeline_mode=resident)]
    args += [norm_g, wg, wu, wd]
    if final_g is not None:
        in_specs.append(pl.BlockSpec((1, D_MODEL), const))
        args.append(final_g)
    return pl.pallas_call(
        functools.partial(_ffn_kernel, has_mix=mix is not None, final=final_g is not None),
        grid=(n // tm,),
        in_specs=in_specs,
        out_specs=pl.BlockSpec((tm, D_MODEL), row),
        out_shape=jax.ShapeDtypeStruct((n, D_MODEL), F32),
        compiler_params=_cp(("parallel",)),
        name="ffn",
    )(*args)


MIX_SLABS = (("rkv", 1152), ("lm", 256), ("bq", 384), ("iq", 512), ("misc", 384), ("pc", 1024))
MIX_COLS = sum(w for _, w in MIX_SLABS)


def _mix_in_kernel(x_ref, g_ref, w_ref, *out_refs):
    h = _rms(x_ref[...], g_ref[...]).astype(BF16)
    dest = [(ref, a) for ref, (_, width) in zip(out_refs, MIX_SLABS) for a in range(0, width, LANES)]
    for c in range(0, MIX_COLS, MXU_N):
        width = min(MXU_N, MIX_COLS - c)
        y = jnp.dot(h, w_ref[:, c:c + width], preferred_element_type=F32)
        for k in range(width // LANES):
            ref, a = dest[c // LANES + k]
            ref[:, a:a + LANES] = y[:, k * LANES:(k + 1) * LANES]


def _mix_in_call(x, norm_g, w_packed):
    n = x.shape[0]
    tm = min(512, n)
    row = lambda i: (i, 0)
    const = lambda i: (0, 0)
    return pl.pallas_call(
        _mix_in_kernel,
        grid=(n // tm,),
        in_specs=[pl.BlockSpec((tm, D_MODEL), row), pl.BlockSpec((1, D_MODEL), const),
                  pl.BlockSpec((D_MODEL, MIX_COLS), const)],
        out_specs=[pl.BlockSpec((tm, w), row) for _, w in MIX_SLABS],
        out_shape=[jax.ShapeDtypeStruct((n, w), F32) for _, w in MIX_SLABS],
        compiler_params=_cp(("parallel",)),
        name="mix_in",
    )(x, norm_g, w_packed)


def _shift_lerp(p, mu, carry_ref, first_tile):
    rows = lax.broadcasted_iota(I32, p.shape, 0)
    carry = jnp.where(first_tile, jnp.zeros_like(carry_ref[0:1, :]), carry_ref[0:1, :])
    prev = jnp.where(rows == 0, carry, pltpu.roll(p, 1, axis=0))
    carry_ref[0:1, :] = p[p.shape[0] - 1:, :]
    return p + (prev - p) * mu


def _rwkv_prep_kernel(*refs, has_vres):
    it = iter(refs)
    rkv_ref, lm_ref = next(it), next(it)
    vfirst_ref = next(it) if has_vres else None
    (mu_rkv_ref, mu_lm_ref, w0_ref, a0_ref, wl_ref, kk_ref, ka_ref, v0_ref, v2_ref,
     ones_ref) = (next(it) for _ in range(10))
    r_ref, kp_ref, v_ref, nk_ref, beta_ref, lw_ref, g_ref = (next(it) for _ in range(7))
    c_rkv_ref, c_lm_ref = next(it), next(it)

    first = pl.program_id(1) == 0
    xa = _shift_lerp(rkv_ref[...], mu_rkv_ref[...], c_rkv_ref, first)
    xl = _shift_lerp(lm_ref[...], mu_lm_ref[...], c_lm_ref, first)
    r = xa[:, 0:A_WIDTH]
    k = xa[:, A_WIDTH:2 * A_WIDTH]
    v = xa[:, 2 * A_WIDTH:3 * A_WIDTH]
    lora = xl[:, 0:LANES]
    lane = lax.broadcasted_iota(I32, lora.shape, 1)
    feat = jnp.where(lane < 32, jnp.tanh(lora), jnp.where(lane < 64, lora, _sigmoid(lora)))
    proj = _dot(feat, wl_ref[...])
    w_pre = w0_ref[...] + proj[:, 0:A_WIDTH]
    sp = jnp.maximum(-w_pre, 0.0) + jnp.log(1.0 + jnp.exp(-jnp.abs(w_pre)))
    lw_ref[...] = -jnp.exp(-sp - 0.5)
    eta = _sigmoid(a0_ref[...] + proj[:, A_WIDTH:2 * A_WIDTH])
    g_ref[...] = proj[:, 2 * A_WIDTH:3 * A_WIDTH]
    kk = k * kk_ref[...]
    ss = _dot_hl_r(kk * kk, ones_ref[...])
    nk = kk / jnp.maximum(jnp.sqrt(ss), 1e-12)
    nk_ref[...] = nk
    beta_ref[...] = nk * eta
    kp_ref[...] = k * (1.0 + (eta - 1.0) * ka_ref[...])
    r_ref[...] = r
    if has_vres:
        mv = xl[:, LANES:2 * LANES]
        mix = _sigmoid(v0_ref[...] + _dot(mv, v2_ref[...]))
        v = v + (vfirst_ref[...] - v) * mix
    v_ref[...] = v


def _rwkv_prep_call(rkv, lm, vfirst, pw, bsz, t):
    tt = min(256, t)
    nt = t // tt
    tok = lambda b, i: (b * nt + i, 0)
    const = lambda b, i: (0, 0)
    has_vres = vfirst is not None
    in_specs = [pl.BlockSpec((tt, 1152), tok), pl.BlockSpec((tt, 256), tok)]
    args = [rkv, lm]
    if has_vres:
        in_specs.append(pl.BlockSpec((tt, A_WIDTH), tok))
        args.append(vfirst)
    params = [pw["mu_rkv"], pw["mu_lm"], pw["w0"], pw["a0"], pw["w_lora"], pw["k_k"], pw["k_a"],
              pw["v0"], pw["v2"], pw["ones_a"]]
    in_specs += [pl.BlockSpec(p.shape, const) for p in params]
    args += params
    n = bsz * t
    out = jax.ShapeDtypeStruct((n, A_WIDTH), F32)
    return pl.pallas_call(
        functools.partial(_rwkv_prep_kernel, has_vres=has_vres),
        grid=(bsz, nt),
        in_specs=in_specs,
        out_specs=[pl.BlockSpec((tt, A_WIDTH), tok)] * 7,
        out_shape=[out] * 7,
        scratch_shapes=[pltpu.VMEM((8, 1152), F32), pltpu.VMEM((8, 256), F32)],
        compiler_params=_cp(("parallel", "arbitrary")),
        name="rwkv_prep",
    )(*args)


def _bd(x, bd_mask):
    return jnp.where(bd_mask, jnp.concatenate([x, x], axis=0), 0.0)


def _chunk_masks():
    r = lax.broadcasted_iota(I32, (PAIR, PAIR), 0)
    c = lax.broadcasted_iota(I32, (PAIR, PAIR), 1)
    same = (r >> 6) == (c >> 6)
    return r, c, same


def _rwkv_core_kernel(r_ref, kp_ref, v_ref, nk_ref, beta_ref, lw_ref, g_ref,
                      lnw_ref, lnb_ref, rk_ref, tri_ref, lvl_ref, ones_ref,
                      o_ref, s_ref):
    @pl.when(pl.program_id(1) == 0)
    def _():
        s_ref[...] = jnp.zeros_like(s_ref)

    row, col, same = _chunk_masks()
    rl, cl = row & (CHUNK - 1), col & (CHUNK - 1)
    strict = same & (rl > cl)
    incl = same & (rl >= cl)
    eye = (row == col).astype(F32)
    tri = tri_ref[...]
    ones_pair = ones_ref[...]
    n_chunks = r_ref.shape[0] // CHUNK

    n_pairs = A_WIDTH // PAIR
    inst = [(ci, p) for ci in range(RWKV_CPI) for p in range(n_pairs)]
    each = lambda f, *cols: [f(*a) for a in zip(*cols)]

    def chunk_body(it, carry):
        rows = [pl.ds(pl.multiple_of((it * RWKV_CPI + ci) * CHUNK, CHUNK), CHUNK) for ci, _ in inst]
        sls = [slice(p * PAIR, (p + 1) * PAIR) for _, p in inst]
        ld = lambda ref: [ref[rw, sl] for rw, sl in zip(rows, sls)]
        r, kp, v, nk, beta, lw = (ld(ref) for ref in (r_ref, kp_ref, v_ref, nk_ref, beta_ref, lw_ref))
        cum = each(lambda x: _dot_hl(tri, x), lw)
        cum_l = each(lambda c: c[CHUNK - 1:CHUNK, :], cum)
        a_bd = each(lambda n, c, w: _bd(-n * jnp.exp(c - w), same), nk, cum, lw)
        r_bd = each(lambda x, c: _bd(x * jnp.exp(c), same), r, cum)
        g_inv = each(lambda c: jnp.exp(-c), cum)
        g_tail = each(lambda c, cl_: jnp.exp(cl_ - c), cum, cum_l)
        b_bd = each(lambda x, g: _bd(x * g, same), beta, g_inv)
        k_bd = each(lambda x, g: _bd(x * g, same), kp, g_inv)
        v_bd = each(lambda x: _bd(x, same), v)
        bg_bd = each(lambda x, g: _bd(x * g, same), beta, g_tail)
        kg_bd = each(lambda x, g: _bd(x * g, same), kp, g_tail)

        ar = each(lambda a, b: jnp.concatenate([a, b], axis=0), a_bd, r_bd)
        bk = each(lambda a, b: jnp.concatenate([a, b], axis=0), b_bd, k_bd)
        gram = each(_dot_nt, ar, bk)
        a_ab = each(lambda g: jnp.where(strict, g[:PAIR, :PAIR], 0.0), gram)
        a_ak = each(lambda g: jnp.where(strict, g[:PAIR, PAIR:], 0.0), gram)
        a_rb = each(lambda g: jnp.where(incl, g[PAIR:, :PAIR], 0.0), gram)
        a_rk = each(lambda g: jnp.where(incl, g[PAIR:, PAIR:], 0.0), gram)

        t_inv = each(lambda a: eye + a * lvl_ref[0], a_ab)
        w_loc = each(_dot, a_ak, v_bd)
        rkv = each(_dot, a_rk, v_bd)
        n_loc = each(_dot_tn, v_bd, kg_bd)
        for lvl in range(1, 6):
            x = each(lambda a, t_: _dot(a * lvl_ref[lvl], t_), a_ab, t_inv)
            t_inv = each(lambda t_, x_: t_ + _dot(t_, x_), t_inv, x)

        au = each(lambda t_, a, w: _dot(t_, jnp.concatenate([a, w], axis=1)), t_inv, a_bd, w_loc)
        ry = each(lambda rb, y0, arb, au_: jnp.concatenate([rb, y0], axis=1) + _dot(arb, au_),
                  r_bd, rkv, a_rb, au)
        mn = each(_dot_tn, au, bg_bd)

        y = [None] * len(inst)
        for ci in range(RWKV_CPI):
            idx = [i for i, (cj, _) in enumerate(inst) if cj == ci]
            s = [s_ref[inst[i][1]] for i in idx]
            y_bd = [_dot_nt(ry[i][:, :PAIR], s_) + ry[i][:, PAIR:] for i, s_ in zip(idx, s)]
            s_new = [s_ * jnp.exp(cum_l[i]) + _dot(s_, mn[i][:PAIR]) + mn[i][PAIR:] + n_loc[i]
                     for i, s_ in zip(idx, s)]
            for i, sn, yb in zip(idx, s_new, y_bd):
                s_ref[inst[i][1]] = sn
                y[i] = yb[:CHUNK] + yb[CHUNK:]

        sums = each(lambda y_, r_, k_, sl: _dot_hl_r(
            jnp.concatenate([y_, r_ * k_ * rk_ref[:, sl]], axis=0), ones_pair), y, r, kp, sls)
        yc = each(lambda y_, s_: y_ - s_[:CHUNK] * (1.0 / HEAD_DIM), y, sums)
        var = each(lambda c: _dot_hl_r(c * c, ones_pair) * (1.0 / HEAD_DIM), yc)
        for i, (rw, sl) in enumerate(zip(rows, sls)):
            yn = yc[i] * lax.rsqrt(var[i] + A_GN_EPS) * lnw_ref[:, sl] + lnb_ref[:, sl]
            o_ref[rw, sl] = ((yn + sums[i][CHUNK:] * v[i]) * g_ref[rw, sl]).astype(o_ref.dtype)
        return carry

    lax.fori_loop(0, n_chunks // RWKV_CPI, chunk_body, 0)


def _rwkv_core_call(coef, pw, consts, bsz, t):
    tt = min(512, t)
    nt = t // tt
    tok = lambda b, i: (b * nt + i, 0)
    c2 = lambda b, i: (0, 0)
    c3 = lambda b, i: (0, 0, 0)
    params = [pw["ln_w"], pw["ln_b"], pw["r_k"]]
    return pl.pallas_call(
        _rwkv_core_kernel,
        grid=(bsz, nt),
        in_specs=([pl.BlockSpec((tt, A_WIDTH), tok)] * 7
                  + [pl.BlockSpec((1, A_WIDTH), c2)] * 3
                  + [pl.BlockSpec((CHUNK, CHUNK), c2), pl.BlockSpec((6, PAIR, PAIR), c3),
                     pl.BlockSpec((PAIR, PAIR), c2)]),
        out_specs=pl.BlockSpec((tt, A_WIDTH), tok),
        out_shape=jax.ShapeDtypeStruct((bsz * t, A_WIDTH), BF16),
        scratch_shapes=[pltpu.VMEM((A_WIDTH // PAIR, PAIR, PAIR), F32)],
        compiler_params=_cp(("parallel", "arbitrary")),
        name="rwkv_core",
    )(*coef, *params, consts["tri"], consts["levels"], consts["ones_pair"])


def _swap_halves(z):
    lane = lax.broadcasted_iota(I32, z.shape, 1)
    return jnp.where((lane & (HEAD_DIM // 2)) == 0,
                     pltpu.roll(z, LANES - HEAD_DIM // 2, axis=1),
                     pltpu.roll(z, HEAD_DIM // 2, axis=1))


def _ret_kernel(pc_ref, cos_ref, sin_ref, dmask_ref, qdec_ref, kdec_ref, cdec_ref, ones_ref,
                o_ref, z_ref):
    @pl.when(pl.program_id(1) == 0)
    def _():
        z_ref[...] = jnp.zeros_like(z_ref)

    _, _, same = _chunk_masks()
    ones_pair = ones_ref[...]
    n_chunks = pc_ref.shape[0] // CHUNK

    n_pairs = C_WIDTH // PAIR
    inst = [(ci, p) for ci in range(RET_CPI) for p in range(n_pairs)]
    each = lambda f, *cols: [f(*a) for a in zip(*cols)]

    def chunk_body(it, carry):
        rows = [pl.ds(pl.multiple_of((it * RET_CPI + ci) * CHUNK, CHUNK), CHUNK) for ci, _ in inst]
        sls = [slice(p * PAIR, (p + 1) * PAIR) for _, p in inst]
        cos = [cos_ref[rw, sl] for rw, sl in zip(rows, sls)]
        sin = [sin_ref[rw, sl] for rw, sl in zip(rows, sls)]
        part = lambda j: [pc_ref[rw, j * C_WIDTH + p * PAIR:j * C_WIDTH + (p + 1) * PAIR]
                          for rw, (_, p) in zip(rows, inst)]
        q, k, v = part(0), part(1), part(2)
        qr = each(lambda z, c, s: z * c + _swap_halves(z) * s, q, cos, sin)
        kr = each(lambda z, c, s: (z * c + _swap_halves(z) * s) * (HEAD_DIM ** -0.5), k, cos, sin)
        q_bd = each(lambda z: _bd(z, same), qr)
        k_bd = each(lambda z: _bd(z, same), kr)
        v_bd = each(lambda z: _bd(z, same), v)
        qd_bd = each(lambda z, sl: _bd(z * qdec_ref[:, sl], same), qr, sls)
        kd_bd = each(lambda z, sl: _bd(z * kdec_ref[:, sl], same), kr, sls)
        scores = each(lambda a, b, ip: _dot_nt(a, b) * dmask_ref[ip[1]], q_bd, k_bd, inst)
        inner = each(_dot, scores, v_bd)
        kv = each(_dot_tn, kd_bd, v_bd)
        y = [None] * len(inst)
        for ci in range(RET_CPI):
            idx = [i for i, (cj, _) in enumerate(inst) if cj == ci]
            z = [z_ref[inst[i][1]] for i in idx]
            cross = [_dot(qd_bd[i], z_) for i, z_ in zip(idx, z)]
            for i, z_, cr in zip(idx, z, cross):
                z_ref[inst[i][1]] = z_ * cdec_ref[:, sls[i]] + kv[i]
                yb = inner[i] + cr
                y[i] = yb[:CHUNK] + yb[CHUNK:]
        ms = each(lambda y_: _dot_hl_r(y_ * y_, ones_pair) * (1.0 / HEAD_DIM), y)
        g = part(3)
        for i, (rw, sl) in enumerate(zip(rows, sls)):
            o_ref[rw, sl] = (g[i] * _sigmoid(g[i]) * y[i] * lax.rsqrt(ms[i] + NORM_EPS)).astype(o_ref.dtype)
        return carry

    lax.fori_loop(0, n_chunks // RET_CPI, chunk_body, 0)


def _ret_call(pc, consts, bsz, t):
    tt = min(512, t)
    nt = t // tt
    tok = lambda b, i: (b * nt + i, 0)
    c2 = lambda b, i: (0, 0)
    return pl.pallas_call(
        _ret_kernel,
        grid=(bsz, nt),
        in_specs=[pl.BlockSpec((tt, 4 * C_WIDTH), tok),
                  pl.BlockSpec((tt, C_WIDTH), lambda b, i: (i, 0)),
                  pl.BlockSpec((tt, C_WIDTH), lambda b, i: (i, 0)),
                  pl.BlockSpec((C_WIDTH // PAIR, PAIR, PAIR), lambda b, i: (0, 0, 0)),
                  pl.BlockSpec((CHUNK, C_WIDTH), c2), pl.BlockSpec((CHUNK, C_WIDTH), c2),
                  pl.BlockSpec((1, C_WIDTH), c2), pl.BlockSpec((PAIR, PAIR), c2)],
        out_specs=pl.BlockSpec((tt, C_WIDTH), tok),
        out_shape=jax.ShapeDtypeStruct((bsz * t, C_WIDTH), BF16),
        scratch_shapes=[pltpu.VMEM((C_WIDTH // PAIR, PAIR, PAIR), F32)],
        compiler_params=_cp(("parallel", "arbitrary")),
        name="retention",
    )(pc, consts["cos"], consts["sin"], consts["ret_dmask"], consts["ret_qdec"],
      consts["ret_kdec"], consts["ret_cdec"], consts["ones_pair"])


def _dsa_kernel(bq_ref, iq_ref, misc_ref, kvn_ref, wuk_ref, wuvt_ref, tri_ref,
                o_ref, c_ref, ct_ref, key_ref, logit_ref, rhs_ref, qls_ref, acc_ref, *, topk):
    j = pl.program_id(1)
    t = misc_ref.shape[0]
    n_units = j + 1
    n_heads = B_HEADS

    @pl.when(j == 0)
    def _():
        def body(i, carry):
            rows = pl.ds(pl.multiple_of(i * QBLK, QBLK), QBLK)
            c = _rms(misc_ref[rows, 0:KV_LATENT], kvn_ref[...])
            c_ref[rows, :] = c.astype(BF16)
            ct_ref[:, rows] = c.T.astype(BF16)
            return carry
        lax.fori_loop(0, t // QBLK, body, 0)

    qrows = pl.ds(pl.multiple_of(j * QB, QB), QB)
    qlat = _dot(bq_ref[...], wuk_ref[...]) * (HEAD_DIM ** -0.5)
    for h in range(n_heads):
        qls_ref[h * QB:(h + 1) * QB, :] = qlat[:, h * LANES:(h + 1) * LANES].astype(BF16)
    iq = iq_ref[...] * (HEAD_DIM ** -0.5)
    lane = lax.broadcasted_iota(I32, (QB, LANES), 1)
    for hp in range(IDX_HEADS // 2):
        slab = iq[:, hp * LANES:(hp + 1) * LANES]
        rhs_ref[hp, 0:QB, :] = jnp.where(lane < HEAD_DIM, slab, 0.0).astype(BF16)
        rhs_ref[hp, QB:2 * QB, :] = jnp.where(lane >= HEAD_DIM, slab, 0.0).astype(BF16)
    iw_t = misc_ref[qrows, 2 * LANES:3 * LANES].T * (IDX_HEADS ** -0.5)

    srow = lax.broadcasted_iota(I32, (QBLK, QB), 0)
    qcol = lax.broadcasted_iota(I32, (QBLK, QB), 1)
    qpos = j * QB + qcol

    def visible(kb):
        return (kb * QBLK + srow) <= qpos

    halves = lambda u: [2 * u, 2 * u + 1]
    blk_rows = lambda kb: pl.ds(pl.multiple_of(kb * QBLK, QBLK), QBLK)
    unit_rows = lambda u: pl.ds(pl.multiple_of(u * 2 * QBLK, 2 * QBLK), 2 * QBLK)

    def walk(body, init):
        carry = lax.fori_loop(0, n_units // 2, lambda i, c: body(2 * i + 1, body(2 * i, c)), init)
        return lax.cond((n_units & 1) == 1, lambda c: body(n_units - 1, c), lambda c: c, carry)

    def score_body(u, carry):
        kbs = halves(u)
        ik2 = [misc_ref[blk_rows(kb), LANES:2 * LANES].astype(BF16) for kb in kbs]
        acc = [jnp.zeros((QBLK, QB), F32) for _ in kbs]
        for hp in range(IDX_HEADS // 2):
            rel = [lax.dot_general(x, rhs_ref[hp], (((1,), (1,)), ((), ())),
                                   preferred_element_type=F32) for x in ik2]
            acc = [a + iw_t[2 * hp:2 * hp + 1, :] * jnp.maximum(r_[:, :QB], 0.0)
                   + iw_t[2 * hp + 1:2 * hp + 2, :] * jnp.maximum(r_[:, QB:], 0.0)
                   for a, r_ in zip(acc, rel)]
        for kb, a in zip(kbs, acc):
            score = jnp.where(visible(kb), a + 0.0, -jnp.inf)
            bits = pltpu.bitcast(score, I32)
            key = jnp.where(bits < 0, bits ^ 0x7FFFFFFF, bits)
            key_ref[blk_rows(kb), :] = key
        return carry
    walk(score_body, 0)

    def count(pred):
        def body(u, acc):
            hit = jnp.where(pred(key_ref[unit_rows(u), :]), 1, 0)
            return acc + jnp.sum(hit.reshape(2 * QBLK // 8, 8, QB), axis=0)
        acc = walk(body, jnp.zeros((8, QB), I32))
        return jnp.sum(acc, axis=0, keepdims=True)

    c0 = count(lambda k: k >= 0)
    tau0 = jnp.where(c0 >= topk, 0, INT_MIN).astype(I32)

    def bit_body(i, tau):
        cand = tau | jnp.left_shift(jnp.int32(1), 30 - i)
        cnt = count(lambda k: k >= cand)
        return jnp.where(cnt >= topk, cand, tau)
    tau = lax.fori_loop(0, 31, bit_body, tau0)
    need = (topk - count(lambda k: k > tau)).astype(F32)

    tri = tri_ref[...]

    def logit_body(u, carry):
        seen, m = carry
        kbs = halves(u)
        keys = [key_ref[blk_rows(kb), :] for kb in kbs]
        eqs = [k == tau for k in keys]
        pref = [jnp.dot(tri, jnp.where(e, 1.0, 0.0).astype(BF16), preferred_element_type=F32)
                for e in eqs]
        logits = lax.dot_general(c_ref[unit_rows(u), :], qls_ref[...], (((1,), (1,)), ((), ())),
                                 preferred_element_type=F32)
        for i, kb in enumerate(kbs):
            rank = seen + pref[i]
            seen = rank[QBLK - 1:QBLK, :]
            sel = visible(kb) & ((keys[i] > tau) | (eqs[i] & (rank <= need)))
            new_m = []
            for h in range(n_heads):
                lh = jnp.where(sel, logits[i * QBLK:(i + 1) * QBLK, h * QB:(h + 1) * QB], NEG_BIG)
                logit_ref[blk_rows(kb), h * QB:(h + 1) * QB] = lh
                new_m.append(jnp.max(lh, axis=0, keepdims=True))
            m = jnp.maximum(m, jnp.concatenate(new_m, axis=1))
        return seen, m

    _, m = walk(logit_body, (jnp.zeros((1, QB), F32), jnp.full((1, n_heads * QB), NEG_BIG, F32)))

    acc_ref[...] = jnp.zeros_like(acc_ref)

    def pv_body(u, lsum):
        rows = unit_rows(u)
        p = jnp.exp(logit_ref[rows, :] - m)
        acc_ref[...] += jnp.dot(ct_ref[:, rows], p.astype(BF16), preferred_element_type=F32)
        return lsum + jnp.sum(p, axis=0, keepdims=True)
    lsum = walk(pv_body, jnp.zeros((1, n_heads * QB), F32))

    o_lat = acc_ref[...] / lsum
    outs = [jnp.dot(wuvt_ref[h], o_lat[:, h * QB:(h + 1) * QB].astype(BF16),
                    preferred_element_type=F32) for h in range(n_heads)]
    o_ref[...] = jnp.concatenate(outs, axis=0).T.astype(o_ref.dtype)


def _dsa_call(bq, iq, misc, dw, consts, bsz, t):
    nq = t // QB
    topk = min(TOPK_MAX, t // 4)
    blk = lambda b, j: (b * nq + j, 0)
    c2 = lambda b, j: (0, 0)
    return pl.pallas_call(
        functools.partial(_dsa_kernel, topk=topk),
        grid=(bsz, nq),
        in_specs=[pl.BlockSpec((QB, B_WIDTH), blk), pl.BlockSpec((QB, 512), blk),
                  pl.BlockSpec((t, 384), lambda b, j: (b, 0)),
                  pl.BlockSpec((1, KV_LATENT), c2),
                  pl.BlockSpec((B_WIDTH, B_HEADS * KV_LATENT), c2),
                  pl.BlockSpec((B_HEADS, HEAD_DIM, KV_LATENT), lambda b, j: (0, 0, 0)),
                  pl.BlockSpec((QBLK, QBLK), c2)],
        out_specs=pl.BlockSpec((QB, B_WIDTH), blk),
        out_shape=jax.ShapeDtypeStruct((bsz * t, B_WIDTH), BF16),
        scratch_shapes=[pltpu.VMEM((t, KV_LATENT), BF16), pltpu.VMEM((KV_LATENT, t), BF16),
                        pltpu.VMEM((t, QB), I32), pltpu.VMEM((t, B_HEADS * QB), F32),
                        pltpu.VMEM((IDX_HEADS // 2, 2 * QB, LANES), BF16),
                        pltpu.VMEM((B_HEADS * QB, KV_LATENT), BF16),
                        pltpu.VMEM((KV_LATENT, B_HEADS * QB), F32)],
        compiler_params=_cp(("parallel", "arbitrary")),
        name="dsa",
    )(bq, iq, misc, dw["kv_norm"], dw["wuk_bd"], dw["wuv_t"], consts["tri128"])


def _pad_cols(w, width):
    return jnp.pad(w, ((0, 0), (0, width - w.shape[1])))


def _pack_w_in(w_in, vres_w_in):
    a, b, c = w_in[:, :1280], w_in[:, 1280:2376], w_in[:, 2376:3400]
    mv = (jnp.zeros((D_MODEL, LANES), F32) if vres_w_in is None else _pad_cols(vres_w_in, LANES))
    ckv, ik, iw = b[:, 384:512], b[:, 1024:1088], b[:, 1088:1096]
    misc = jnp.concatenate([ckv, ik, ik, _pad_cols(iw, LANES)], axis=1)
    packed = jnp.concatenate([a[:, :1152], a[:, 1152:1280], mv, b[:, :384], b[:, 512:1024], misc, c],
                             axis=1)
    return packed.astype(BF16)


def _row(v):
    return v.reshape(1, -1).astype(F32)


def _block_diag(blocks):
    n = len(blocks)
    r, c = blocks[0].shape
    out = jnp.zeros((n * r, n * c), blocks[0].dtype)
    for i, blk in enumerate(blocks):
        out = out.at[i * r:(i + 1) * r, i * c:(i + 1) * c].set(blk)
    return out


def _constants(t):
    consts = {}
    i64 = jnp.arange(CHUNK)
    consts["tri"] = (i64[:, None] >= i64[None, :]).astype(BF16)
    i128 = jnp.arange(PAIR)
    consts["tri128"] = (i128[:, None] >= i128[None, :]).astype(BF16)
    consts["ones_pair"] = ((i128[:, None] // HEAD_DIM) == (i128[None, :] // HEAD_DIM)).astype(BF16)
    i384 = jnp.arange(A_WIDTH)
    consts["ones_a"] = ((i384[:, None] // HEAD_DIM) == (i384[None, :] // HEAD_DIM)).astype(BF16)
    r, c = i128[:, None], i128[None, :]
    lv = []
    for lvl in range(6):
        s = 1 << lvl
        lv.append(((r // (2 * s)) == (c // (2 * s))) & ((r % (2 * s)) >= s) & ((c % (2 * s)) < s))
    consts["levels"] = jnp.stack(lv).astype(F32)
    half = HEAD_DIM // 2
    theta = ROPE_BASE ** (-jnp.linspace(0.0, 1.0, half, dtype=F32))
    ang = jnp.arange(t, dtype=F32)[:, None] * theta[None, :]
    cos, sin = jnp.cos(ang), jnp.sin(ang)
    n_heads = C_WIDTH // HEAD_DIM
    consts["cos"] = jnp.tile(jnp.concatenate([cos, cos], axis=1), (1, n_heads))
    consts["sin"] = jnp.tile(jnp.concatenate([-sin, sin], axis=1), (1, n_heads))
    log_gamma = jnp.log(1.0 - 2.0 ** (-5.0 - jnp.arange(n_heads, dtype=F32)))
    n = jnp.arange(CHUNK, dtype=F32)
    diff = n[:, None] - n[None, :]
    intra = jnp.where(diff[None] >= 0,
                      jnp.exp(jnp.maximum(diff, 0.0)[None] * log_gamma[:, None, None]), 0.0)
    consts["ret_dmask"] = jnp.stack([_block_diag([intra[2 * p], intra[2 * p + 1]])
                                     for p in range(n_heads // 2)])
    lane_head = jnp.repeat(jnp.arange(n_heads), HEAD_DIM)
    consts["ret_qdec"] = jnp.exp((n[:, None] + 1.0) * log_gamma[None, :])[:, lane_head]
    consts["ret_kdec"] = jnp.exp((CHUNK - 1.0 - n)[:, None] * log_gamma[None, :])[:, lane_head]
    consts["ret_cdec"] = jnp.exp(CHUNK * log_gamma)[None, lane_head]
    return consts


def kernel(x, ffn1_norm, ffn1_w_gate, ffn1_w_up, ffn1_w_down, mix_norm, w_in, w_out, rwkv_mu, rwkv_w0, rwkv_w2, rwkv_a0, rwkv_a2, rwkv_g2, rwkv_k_k, rwkv_k_a, rwkv_r_k, rwkv_ln_w, rwkv_ln_b, rwkv_vres_w_in, rwkv_vres_mu, rwkv_v0, rwkv_v2, dsa_kv_norm, dsa_w_uk, dsa_w_uv, ffn2_norm, ffn2_w_gate, ffn2_w_up, ffn2_w_down, final_norm):
    bsz, t, _ = x.shape
    depth = w_in.shape[0]
    consts = _constants(t)
    xf = x.reshape(bsz * t, D_MODEL)
    ffn1_w = (ffn1_w_gate, ffn1_w_up, ffn1_w_down)
    ffn2_w = (ffn2_w_gate, ffn2_w_up, ffn2_w_down)
    v_first = None
    for l in range(depth):
        xf = _ffn_call(xf, _row(ffn1_norm[l]), ffn1_w, l)
        w_packed = _pack_w_in(w_in[l], None if l == 0 else rwkv_vres_w_in[l - 1])
        rkv, lm, bq, iq, misc, pc = _mix_in_call(xf, _row(mix_norm[l]), w_packed)

        zeros_a = jnp.zeros((1, A_WIDTH), F32)
        w_lora = jnp.zeros((LANES, 3 * A_WIDTH), F32)
        w_lora = w_lora.at[0:32, 0:A_WIDTH].set(rwkv_w2[l])
        w_lora = w_lora.at[32:64, A_WIDTH:2 * A_WIDTH].set(rwkv_a2[l])
        w_lora = w_lora.at[64:128, 2 * A_WIDTH:].set(rwkv_g2[l])
        mu = rwkv_mu[l]
        mu_mv = jnp.zeros((LANES,), F32) if l == 0 else jnp.pad(rwkv_vres_mu[l - 1], (0, LANES - 16))
        pw = {
            "mu_rkv": _row(mu[:1152]),
            "mu_lm": _row(jnp.concatenate([mu[1152:1280], mu_mv])),
            "w0": _row(rwkv_w0[l]), "a0": _row(rwkv_a0[l]), "w_lora": w_lora.astype(BF16),
            "k_k": _row(rwkv_k_k[l]), "k_a": _row(rwkv_k_a[l]),
            "v0": zeros_a if l == 0 else _row(rwkv_v0[l - 1]),
            "v2": (jnp.zeros((LANES, A_WIDTH), BF16) if l == 0
                   else jnp.pad(rwkv_v2[l - 1], ((0, LANES - 16), (0, 0))).astype(BF16)),
            "ones_a": consts["ones_a"],
            "ln_w": _row(rwkv_ln_w[l]), "ln_b": _row(rwkv_ln_b[l]), "r_k": _row(rwkv_r_k[l]),
        }
        coef = _rwkv_prep_call(rkv, lm, v_first, pw, bsz, t)
        if l == 0:
            v_first = coef[2]
        o_a = _rwkv_core_call(coef, pw, consts, bsz, t)

        dw = {
            "kv_norm": _row(dsa_kv_norm[l]),
            "wuk_bd": _block_diag([dsa_w_uk[l, h] for h in range(B_HEADS)]).astype(BF16),
            "wuv_t": jnp.swapaxes(dsa_w_uv[l], 1, 2).astype(BF16),
        }
        o_b = _dsa_call(bq, iq, misc, dw, consts, bsz, t)

        o_c = _ret_call(pc, consts, bsz, t)

        wo = w_out[l].astype(BF16)
        mix = (o_a, o_b, o_c, wo[:A_WIDTH], wo[A_WIDTH:A_WIDTH + B_WIDTH], wo[A_WIDTH + B_WIDTH:])
        xf = _ffn_call(xf, _row(ffn2_norm[l]), ffn2_w, l,
                       mix=mix, final_g=_row(final_norm) if l == depth - 1 else None)
    return xf.reshape(bsz, t, D_MODEL)
```
